```python
import jax, jax.numpy as jnp
from jax import lax
import numpy as np

D_MODEL = 4096
BATCH = 8
SEQ = 2048
DEPTH = 4

CHUNK = 64
H_A = 8
DK_A = 128
DV_A = 128
W_A = H_A * DK_A
H_B = 8
W_B = 1024
BW_B = W_B // H_B
CONV_K = 4
C_RG = 8.0
D_FF = 4 * D_MODEL
SPLITS = (W_A, 2 * W_A, 3 * W_A, 4 * W_A, 4 * W_A + W_B, 4 * W_A + 2 * W_B,
          4 * W_A + 2 * W_B + D_MODEL)
IN_COLS = 4 * W_A + 2 * W_B + 2 * D_MODEL
ALPHA = (2.0 * DEPTH) ** 0.25
BETA = (8.0 * DEPTH) ** -0.25
EPS = 1e-5
ROPE_BASE = 10000.0

kernel_name = "hybrid_retention_rglru_deepnorm_trunk"


def layer_norm(x, g, b):
    xf = x.astype(jnp.float32)
    mu = jnp.mean(xf, axis=-1, keepdims=True)
    var = jnp.mean(jnp.square(xf - mu), axis=-1, keepdims=True)
    return ((xf - mu) * lax.rsqrt(var + EPS) * g + b).astype(x.dtype)


def rotary(t, cos, sin):
    t1, t2 = jnp.split(t.astype(jnp.float32), 2, axis=-1)
    c = cos[None, :, None, :]
    s = sin[None, :, None, :]
    return jnp.concatenate([t1 * c - t2 * s, t2 * c + t1 * s], axis=-1)


def retention(q, k, v):
    b_, s_, h, dk = q.shape
    dv = v.shape[-1]
    nc = s_ // CHUNK
    lg = jnp.log1p(-jnp.exp2(-5.0 - jnp.arange(h, dtype=jnp.float32)))
    idx = jnp.arange(CHUNK, dtype=jnp.float32)
    intra_decay = jnp.exp(lg[:, None, None] * jnp.abs(idx[:, None] - idx[None, :]))
    q_decay = jnp.exp(lg[:, None] * (idx + 1.0))
    k_decay = jnp.exp(lg[:, None] * (CHUNK - 1.0 - idx))
    chunk_decay = jnp.exp(lg * CHUNK)
    qc = q.astype(jnp.float32).reshape(b_, nc, CHUNK, h, dk)
    kc = k.astype(jnp.float32).reshape(b_, nc, CHUNK, h, dk) * (dk ** -0.5)
    vc = v.astype(jnp.float32).reshape(b_, nc, CHUNK, h, dv)
    scores = jnp.einsum('bnihd,bnjhd->bhnij', qc, kc) * intra_decay[:, None]
    o_intra = jnp.einsum('bhnij,bnjhe->bnihe', scores, vc)
    kv = jnp.einsum('bnjhd,hj,bnjhe->nbhde', kc, k_decay, vc)

    def step(state, kv_c):
        return chunk_decay[None, :, None, None] * state + kv_c, state

    _, s_prev = lax.scan(step, jnp.zeros_like(kv[0]), kv)
    o_inter = jnp.einsum('bnihd,hi,nbhde->bnihe', qc, q_decay, s_prev)
    return (o_intra + o_inter).reshape(b_, s_, h, dv)


def head_group_norm(o, gain):
    mu = jnp.mean(o, axis=-1, keepdims=True)
    var = jnp.mean(jnp.square(o - mu), axis=-1, keepdims=True)
    y = (o - mu) * lax.rsqrt(var + EPS)
    return y.reshape(o.shape[0], o.shape[1], -1) * gain


def causal_conv(x, w, b):
    s_ = x.shape[1]
    xp = jnp.pad(x, ((0, 0), (CONV_K - 1, 0), (0, 0)))
    out = xp[:, 0:s_] * w[0]
    for tap in range(1, CONV_K):
        out = out + xp[:, tap:tap + s_] * w[tap]
    return out + b


def rg_lru(x, w_a, b_a, w_x, b_x, lam):
    b_, s_, wd = x.shape
    xh = x.reshape(b_, s_, H_B, BW_B)
    r = jax.nn.sigmoid(jnp.einsum('bshi,hij->bshj', xh, w_a).reshape(b_, s_, wd) + b_a)
    i = jax.nn.sigmoid(jnp.einsum('bshi,hij->bshj', xh, w_x).reshape(b_, s_, wd) + b_x)
    log_a = -C_RG * r.astype(jnp.float32) * jax.nn.softplus(-lam.astype(jnp.float32))
    a = jnp.exp(log_a)
    u = jnp.sqrt(-jnp.expm1(2.0 * log_a)) * (i * x).astype(jnp.float32)

    def combine(lhs, rhs):
        a1, b1 = lhs
        a2, b2 = rhs
        return a1 * a2, a2 * b1 + b2

    _, h = lax.associative_scan(combine, (a, u), axis=1)
    return h.astype(x.dtype)


def setup_inputs(seed: int = 0) -> dict:
    key = jax.random.key(seed)
    ks = jax.random.split(key, 20)
    f32 = jnp.float32
    nrm = lambda k, shape, scale: jax.random.normal(k, shape, f32) * scale
    x = jax.random.normal(ks[0], (BATCH, SEQ, D_MODEL), f32)
    w_in = nrm(ks[1], (DEPTH, D_MODEL, IN_COLS), D_MODEL ** -0.5)
    ret_gn_w = 1.0 + nrm(ks[2], (DEPTH, W_A), 0.02)
    conv_w = nrm(ks[3], (DEPTH, CONV_K, W_B), CONV_K ** -0.5)
    conv_b = nrm(ks[4], (DEPTH, W_B), 0.01)
    rg_wa = nrm(ks[5], (DEPTH, H_B, BW_B, BW_B), BW_B ** -0.5)
    rg_ba = nrm(ks[6], (DEPTH, W_B), 0.01)
    rg_wx = nrm(ks[7], (DEPTH, H_B, BW_B, BW_B), BW_B ** -0.5)
    rg_bx = nrm(ks[8], (DEPTH, W_B), 0.01)
    u = jax.random.uniform(ks[9], (DEPTH, W_B), f32, 0.9, 0.999)
    p = u ** (1.0 / C_RG)
    rg_lambda = jnp.log(p) - jnp.log1p(-p)
    w_br_a = nrm(ks[10], (DEPTH, W_A, D_MODEL), (W_A ** -0.5) * BETA)
    w_br_b = nrm(ks[11], (DEPTH, W_B, D_MODEL), (W_B ** -0.5) * BETA)
    b_gate = nrm(ks[12], (DEPTH, 2 * D_MODEL), 0.01)
    w_o = nrm(ks[13], (DEPTH, D_MODEL, D_MODEL), (D_MODEL ** -0.5) * BETA)
    ln1_g = 1.0 + nrm(ks[14], (DEPTH, D_MODEL), 0.02)
    ln1_b = nrm(ks[15], (DEPTH, D_MODEL), 0.01)
    w_up = nrm(ks[16], (DEPTH, D_MODEL, D_FF), D_MODEL ** -0.5)
    w_down = nrm(ks[17], (DEPTH, D_FF, D_MODEL), (D_FF ** -0.5) * BETA)
    ln2_g = 1.0 + nrm(ks[18], (DEPTH, D_MODEL), 0.02)
    ln2_b = nrm(ks[19], (DEPTH, D_MODEL), 0.01)
    return {"x": x, "w_in": w_in, "ret_gn_w": ret_gn_w, "conv_w": conv_w,
            "conv_b": conv_b, "rg_wa": rg_wa, "rg_ba": rg_ba, "rg_wx": rg_wx,
            "rg_bx": rg_bx, "rg_lambda": rg_lambda, "w_br_a": w_br_a,
            "w_br_b": w_br_b, "b_gate": b_gate, "w_o": w_o, "ln1_g": ln1_g,
            "ln1_b": ln1_b, "w_up": w_up, "w_down": w_down, "ln2_g": ln2_g,
            "ln2_b": ln2_b}


def reference(x, w_in, ret_gn_w, conv_w, conv_b, rg_wa, rg_ba, rg_wx, rg_bx,
              rg_lambda, w_br_a, w_br_b, b_gate, w_o, ln1_g, ln1_b, w_up, w_down,
              ln2_g, ln2_b):
    b_, s_, _ = x.shape
    pos = jnp.arange(s_, dtype=jnp.float32)
    inv_freq = ROPE_BASE ** (-jnp.arange(0, DK_A, 2, dtype=jnp.float32) / DK_A)
    ang = pos[:, None] * inv_freq[None, :]
    cos, sin = jnp.cos(ang), jnp.sin(ang)

    for l in range(DEPTH):
        proj = x @ w_in[l]
        q, k, v, g_ret, xr, gr, mg_a, mg_b = jnp.split(proj, SPLITS, axis=-1)
        qh = rotary(q.reshape(b_, s_, H_A, DK_A), cos, sin)
        kh = rotary(k.reshape(b_, s_, H_A, DK_A), cos, sin)
        vh = v.reshape(b_, s_, H_A, DV_A)
        o = head_group_norm(retention(qh, kh, vh), ret_gn_w[l]).astype(x.dtype)
        y_a = (jax.nn.silu(g_ret) * o) @ w_br_a[l]
        xc = causal_conv(xr, conv_w[l], conv_b[l])
        hr = rg_lru(xc, rg_wa[l], rg_ba[l], rg_wx[l], rg_bx[l], rg_lambda[l])
        y_b = (hr * jax.nn.gelu(gr)) @ w_br_b[l]
        bg_a, bg_b = jnp.split(b_gate[l], 2)
        merged = jax.nn.sigmoid(mg_a + bg_a) * y_a + jax.nn.sigmoid(mg_b + bg_b) * y_b
        x = layer_norm(ALPHA * x + merged @ w_o[l], ln1_g[l], ln1_b[l])
        ff = jnp.square(jax.nn.relu(x @ w_up[l])) @ w_down[l]
        x = layer_norm(ALPHA * x + ff, ln2_g[l], ln2_b[l])
    return x
```

```python
import functools
import math

import jax
import jax.numpy as jnp
from jax import lax
from jax.experimental import pallas as pl
from jax.experimental.pallas import tpu as pltpu

F32 = jnp.float32
BF16 = jnp.bfloat16

H_A = 8
DK_A = 128
W_A = H_A * DK_A
H_B = 8
BW_B = 128
W_B = H_B * BW_B
CONV_K = 4
C_RG = 8.0
CHUNK = 64
EPS = 1e-5
ROPE_BASE = 10000.0

V7X_VMEM_BYTES = 64 * 1024 * 1024
V7X_VMEM_REQUEST_CAP = 60 * 1024 * 1024
LANES = 128
SUBLANES = 8

RET_BLOCK = 256
SCAN_TILE = 256


def _nbytes(shape, dtype):
    return math.prod(shape) * jnp.dtype(dtype).itemsize


def _params(semantics, block_bytes, scratch_bytes=0, temp_bytes=0):
    need = 2 * block_bytes + scratch_bytes + temp_bytes + 4 * 1024 * 1024
    limit = min(max(need, 32 * 1024 * 1024), V7X_VMEM_REQUEST_CAP)
    return pltpu.CompilerParams(dimension_semantics=semantics, vmem_limit_bytes=limit)


def _tile(dim, pref):
    t = min(dim, pref)
    while dim % t:
        t //= 2
    return t


def _fused_matmul(name, lhs, dots, extras, epilogue, n_out, out_dtype, bm, bn, rhs_resident):
    m = lhs[0].shape[0]
    assert m % bm == 0 and n_out % bn == 0
    if rhs_resident:
        grid = (n_out // bn, m // bm)
        ij = lambda g0, g1: (g1, g0)
    else:
        grid = (m // bm, n_out // bn)
        ij = lambda g0, g1: (g0, g1)

    in_specs, args, block_bytes = [], [], 0
    for a in lhs:
        k = a.shape[1]
        in_specs.append(pl.BlockSpec((bm, k), lambda g0, g1: (ij(g0, g1)[0], 0)))
        args.append(a)
        block_bytes += _nbytes((bm, k), a.dtype)
    for (_, w, lead, off) in dots:
        k = w.shape[-2]
        if w.ndim == 3:
            spec = pl.BlockSpec((None, k, bn),
                                lambda g0, g1, lead=lead, off=off: (lead, 0, off + ij(g0, g1)[1]))
        else:
            spec = pl.BlockSpec((k, bn), lambda g0, g1, off=off: (0, off + ij(g0, g1)[1]))
        in_specs.append(spec)
        args.append(w)
        block_bytes += _nbytes((k, bn), w.dtype)
    for (e, bshape, imap) in extras:
        in_specs.append(pl.BlockSpec(bshape, lambda g0, g1, imap=imap: imap(*ij(g0, g1))))
        args.append(e)
        block_bytes += _nbytes(bshape, e.dtype)
    block_bytes += _nbytes((bm, bn), out_dtype)

    n_lhs, n_dots = len(lhs), len(dots)

    def body(*refs):
        a_refs = refs[:n_lhs]
        w_refs = refs[n_lhs:n_lhs + n_dots]
        e_refs = refs[n_lhs + n_dots:-1]
        o_ref = refs[-1]
        i, j = ij(pl.program_id(0), pl.program_id(1))
        accs = [jnp.dot(a_refs[d[0]][...], w_ref[...], preferred_element_type=F32)
                for d, w_ref in zip(dots, w_refs)]
        epilogue(accs, e_refs, o_ref, i, j)

    return pl.pallas_call(
        body,
        out_shape=jax.ShapeDtypeStruct((m, n_out), out_dtype),
        grid=grid,
        in_specs=in_specs,
        out_specs=pl.BlockSpec((bm, bn), lambda g0, g1: ij(g0, g1)),
        compiler_params=_params(("parallel", "parallel"), block_bytes,
                                temp_bytes=(n_dots + 1) * bm * bn * 4),
        name=name,
    )(*args)


def _gelu_tanh(x):
    c = math.sqrt(2.0 / math.pi)
    return 0.5 * x * (1.0 + jnp.tanh(c * (x + 0.044715 * (x * x * x))))


def _sigmoid(x):
    return 1.0 / (1.0 + jnp.exp(-x))


def _retention_body(q_ref, k_ref, v_ref, g_ref, d_ref, qd_ref, kd_ref, gain_ref, o_ref):
    seq = q_ref.shape[0]
    blk = d_ref.shape[0]
    dmat = d_ref[...]
    qdec = qd_ref[...]
    kdec = kd_ref[...]
    block_decay = qdec[blk - 1:blk, :]
    gain = gain_ref[...]
    state = jnp.zeros((DK_A, DK_A), F32)
    for n in range(seq // blk):
        rows = pl.ds(n * blk, blk)
        q = q_ref[rows, :]
        k = k_ref[rows, :]
        v = v_ref[rows, :]
        s = lax.dot_general(q, k, (((1,), (1,)), ((), ())), preferred_element_type=F32)
        o = jnp.dot((s * dmat).astype(BF16), v, preferred_element_type=F32)
        q_in = (q.astype(F32) * qdec).astype(BF16)
        o = o + jnp.dot(q_in, state.astype(BF16), preferred_element_type=F32)
        k_out_t = (k.astype(F32) * kdec).T.astype(BF16)
        state = state * block_decay + jnp.dot(k_out_t, v, preferred_element_type=F32)
        mu = jnp.mean(o, axis=-1, keepdims=True)
        d = o - mu
        var = jnp.mean(d * d, axis=-1, keepdims=True)
        y = d * lax.rsqrt(var + EPS) * gain
        o_ref[rows, :] = (g_ref[rows, :].astype(F32) * y).astype(o_ref.dtype)


def _retention(qkvg, dmat, qdec, kdec, gain, batch, seq):
    m = qkvg.shape[0]
    blk = dmat.shape[1]
    hb = W_A // DK_A
    col = lambda c: (lambda b, h: (b, c * hb + h))
    head = lambda b, h: (h, 0, 0)
    block_bytes = 5 * _nbytes((seq, DK_A), BF16) + _nbytes((blk, blk), F32) + 2 * _nbytes((blk, DK_A), F32)
    return pl.pallas_call(
        _retention_body,
        out_shape=jax.ShapeDtypeStruct((m, W_A), BF16),
        grid=(batch, H_A),
        in_specs=[pl.BlockSpec((seq, DK_A), col(0)),
                  pl.BlockSpec((seq, DK_A), col(1)),
                  pl.BlockSpec((seq, DK_A), col(2)),
                  pl.BlockSpec((seq, DK_A), col(3)),
                  pl.BlockSpec((None, blk, blk), head),
                  pl.BlockSpec((None, blk, DK_A), head),
                  pl.BlockSpec((None, blk, DK_A), head),
                  pl.BlockSpec((1, DK_A), lambda b, h: (0, h))],
        out_specs=pl.BlockSpec((seq, DK_A), lambda b, h: (b, h)),
        compiler_params=_params(("parallel", "parallel"), block_bytes, temp_bytes=8 * blk * blk * 4),
        name="retention",
    )(qkvg, qkvg, qkvg, qkvg, dmat, qdec, kdec, gain)


def _rglru_body(x_ref, gg_ref, cw_ref, cb_ref, wa_ref, ba_ref, wx_ref, bx_ref, lam_ref, o_ref,
                xpad, a_s, u_s, h_s):
    nb, tt, _ = x_ref.shape
    t = pl.program_id(1)

    @pl.when(t == 0)
    def _():
        xpad[:, 0:SUBLANES, :] = jnp.zeros((nb, SUBLANES, BW_B), F32)
        h_s[...] = jnp.zeros_like(h_s)

    xpad[:, SUBLANES:, :] = x_ref[...]
    cw = cw_ref[...]
    xc = cb_ref[...][None, :, :]
    for tap in range(CONV_K):
        lo = SUBLANES - (CONV_K - 1) + tap
        xc = xc + xpad[:, lo:lo + tt, :] * cw[tap:tap + 1, :][None, :, :]
    xpad[:, 0:SUBLANES, :] = xpad[:, tt:tt + SUBLANES, :]

    xc2 = xc.reshape(nb * tt, BW_B)
    xb = xc2.astype(BF16)
    r = _sigmoid(jnp.dot(xb, wa_ref[...], preferred_element_type=F32) + ba_ref[...])
    ig = _sigmoid(jnp.dot(xb, wx_ref[...], preferred_element_type=F32) + bx_ref[...])
    neg_lam = -lam_ref[...]
    softplus = jnp.maximum(neg_lam, 0.0) + jnp.log(1.0 + jnp.exp(-jnp.abs(neg_lam)))
    a = jnp.exp((-C_RG * softplus) * r)
    a_s[...] = a
    u_s[...] = jnp.sqrt(1.0 - a * a) * (ig * xc2)

    def step(s, h):
        rows = pl.ds(s, nb, stride=tt)
        h = a_s[rows, :] * h + u_s[rows, :]
        u_s[rows, :] = h
        return h

    h_s[...] = lax.fori_loop(0, tt, step, h_s[...], unroll=8)
    o_ref[...] = (u_s[...].reshape(nb, tt, BW_B) * gg_ref[...]).astype(o_ref.dtype)


def _rglru(xg, conv_w, conv_b, wa, ba, wx, bx, lam, batch, seq):
    tt = _tile(seq, SCAN_TILE)
    xg3 = xg.reshape(batch, seq, 2 * W_B)
    hb = W_B // BW_B
    vec = pl.BlockSpec((1, BW_B), lambda h, t: (0, h))
    block_bytes = 2 * _nbytes((batch, tt, BW_B), F32) + _nbytes((batch, tt, BW_B), BF16) \
        + 2 * _nbytes((BW_B, BW_B), BF16)
    scratch = [pltpu.VMEM((batch, tt + SUBLANES, BW_B), F32),
               pltpu.VMEM((batch * tt, BW_B), F32),
               pltpu.VMEM((batch * tt, BW_B), F32),
               pltpu.VMEM((batch, BW_B), F32)]
    scratch_bytes = 3 * _nbytes((batch, tt + SUBLANES, BW_B), F32)
    out = pl.pallas_call(
        _rglru_body,
        out_shape=jax.ShapeDtypeStruct((batch, seq, W_B), BF16),
        grid=(H_B, seq // tt),
        in_specs=[pl.BlockSpec((batch, tt, BW_B), lambda h, t: (0, t, h)),
                  pl.BlockSpec((batch, tt, BW_B), lambda h, t: (0, t, hb + h)),
                  pl.BlockSpec((CONV_K, BW_B), lambda h, t: (0, h)),
                  vec,
                  pl.BlockSpec((None, BW_B, BW_B), lambda h, t: (h, 0, 0)),
                  vec,
                  pl.BlockSpec((None, BW_B, BW_B), lambda h, t: (h, 0, 0)),
                  vec,
                  vec],
        out_specs=pl.BlockSpec((batch, tt, BW_B), lambda h, t: (0, t, h)),
        scratch_shapes=scratch,
        compiler_params=_params(("parallel", "arbitrary"), block_bytes, scratch_bytes,
                                temp_bytes=8 * batch * tt * BW_B * 4),
        name="rglru",
    )(xg3, xg3, conv_w, conv_b, wa, ba, wx, bx, lam)
    return out.reshape(batch * seq, W_B)


def _layernorm_body(y_ref, g_ref, b_ref, o32_ref, o16_ref):
    y = y_ref[...]
    mu = jnp.mean(y, axis=-1, keepdims=True)
    d = y - mu
    var = jnp.mean(d * d, axis=-1, keepdims=True)
    out = d * lax.rsqrt(var + EPS) * g_ref[...] + b_ref[...]
    o32_ref[...] = out
    o16_ref[...] = out.astype(BF16)


def _layernorm(y, g, b):
    m, d = y.shape
    bm = _tile(m, 256)
    row = pl.BlockSpec((bm, d), lambda i: (i, 0))
    vec = pl.BlockSpec((1, d), lambda i: (0, 0))
    block_bytes = 2 * _nbytes((bm, d), F32) + _nbytes((bm, d), BF16)
    return pl.pallas_call(
        _layernorm_body,
        out_shape=(jax.ShapeDtypeStruct((m, d), F32), jax.ShapeDtypeStruct((m, d), BF16)),
        grid=(m // bm,),
        in_specs=[row, vec, vec],
        out_specs=(row, row),
        compiler_params=_params(("parallel",), block_bytes, temp_bytes=2 * bm * d * 4),
        name="layernorm",
    )(y, g, b)


def _matmul_residual_ktiled(name, a, w, lead, resid, alpha, bm, bn, bk):
    m, kdim = a.shape
    n = w.shape[-1]
    nk = kdim // bk

    def body(a_ref, w_ref, r_ref, o_ref, acc_ref):
        k = pl.program_id(2)

        @pl.when(k == 0)
        def _():
            acc_ref[...] = jnp.zeros_like(acc_ref)

        acc_ref[...] += jnp.dot(a_ref[...], w_ref[...], preferred_element_type=F32)

        @pl.when(k == nk - 1)
        def _():
            o_ref[...] = alpha * r_ref[...] + acc_ref[...]

    block_bytes = _nbytes((bm, bk), BF16) + _nbytes((bk, bn), BF16) + 2 * _nbytes((bm, bn), F32)
    return pl.pallas_call(
        body,
        out_shape=jax.ShapeDtypeStruct((m, n), F32),
        grid=(m // bm, n // bn, nk),
        in_specs=[pl.BlockSpec((bm, bk), lambda i, j, k: (i, k)),
                  pl.BlockSpec((None, bk, bn), lambda i, j, k: (lead, k, j)),
                  pl.BlockSpec((bm, bn), lambda i, j, k: (i, j))],
        out_specs=pl.BlockSpec((bm, bn), lambda i, j, k: (i, j)),
        scratch_shapes=[pltpu.VMEM((bm, bn), F32)],
        compiler_params=_params(("parallel", "parallel", "arbitrary"), block_bytes,
                                scratch_bytes=bm * bn * 4, temp_bytes=bm * bn * 4),
        name=name,
    )(a, w, resid)


def _retention_tables(blk):
    lg = jnp.log1p(-jnp.exp2(-5.0 - jnp.arange(H_A, dtype=F32)))
    idx = jnp.arange(blk, dtype=F32)
    diff = jnp.abs(idx[:, None] - idx[None, :])
    chunk = jnp.arange(blk) // CHUNK
    visible = chunk[None, :] <= chunk[:, None]
    dmat = jnp.where(visible[None], jnp.exp(lg[:, None, None] * diff[None]), 0.0)
    qdec = jnp.exp(lg[:, None] * (idx + 1.0))
    kdec = jnp.exp(lg[:, None] * (blk - 1.0 - idx))
    bcast = lambda t: jnp.broadcast_to(t[:, :, None], (H_A, blk, DK_A))
    return dmat.astype(F32), bcast(qdec), bcast(kdec)


def _rotary_tables(seq):
    pos = jnp.arange(seq, dtype=F32)
    inv_freq = ROPE_BASE ** (-jnp.arange(0, DK_A, 2, dtype=F32) / DK_A)
    ang = pos[:, None] * inv_freq[None, :]
    cos, sin = jnp.cos(ang), jnp.sin(ang)
    return jnp.concatenate([cos, cos], axis=-1), jnp.concatenate([-sin, sin], axis=-1)


def kernel(x, w_in, ret_gn_w, conv_w, conv_b, rg_wa, rg_ba, rg_wx, rg_bx, rg_lambda, w_br_a, w_br_b,
           b_gate, w_o, ln1_g, ln1_b, w_up, w_down, ln2_g, ln2_b):
    batch, seq, d_model = x.shape
    depth = w_in.shape[0]
    d_ff = w_up.shape[-1]
    m = batch * seq
    alpha = (2.0 * depth) ** 0.25
    assert w_in.shape[-1] == 4 * W_A + 2 * W_B + 2 * d_model
    assert seq % RET_BLOCK == 0 or seq < RET_BLOCK
    assert batch == SUBLANES, "the scan packs the batch onto the sublanes of one vreg"

    blk = _tile(seq, RET_BLOCK)
    dmat, qdec, kdec = _retention_tables(blk)
    cos2, sin2 = _rotary_tables(seq)

    w_in_b = w_in.astype(BF16)
    w_br_a_b = w_br_a.astype(BF16)
    w_br_b_b = w_br_b.astype(BF16)
    w_o_b = w_o.astype(BF16)
    w_up_b = w_up.astype(BF16)
    w_down_b = w_down.astype(BF16)
    rg_wa_b = rg_wa.astype(BF16)
    rg_wx_b = rg_wx.astype(BF16)

    bm = _tile(seq, 1024)
    bn = 1024
    pos_blocks = seq // bm
    k_scale = DK_A ** -0.5

    def qkvg_epilogue(accs, e_refs, o_ref, i, j):
        acc = accs[0]
        cos_ref, sin_ref = e_refs

        def rotary(scale):
            c, s = cos_ref[...], sin_ref[...]
            for hh in range(bn // DK_A):
                cols = slice(hh * DK_A, (hh + 1) * DK_A)
                t = acc[:, cols]
                rot = t * c + pltpu.roll(t, DK_A // 2, 1) * s
                o_ref[:, cols] = (rot * scale).astype(o_ref.dtype) if scale != 1.0 else rot.astype(o_ref.dtype)

        @pl.when(j == 0)
        def _():
            rotary(1.0)

        @pl.when(j == 1)
        def _():
            rotary(k_scale)

        @pl.when(j == 2)
        def _():
            o_ref[...] = acc.astype(o_ref.dtype)

        @pl.when(j == 3)
        def _():
            o_ref[...] = (acc * _sigmoid(acc)).astype(o_ref.dtype)

    def xg_epilogue(accs, e_refs, o_ref, i, j):
        acc = accs[0]

        @pl.when(j == 0)
        def _():
            o_ref[...] = acc

        @pl.when(j == 1)
        def _():
            o_ref[...] = _gelu_tanh(acc)

    def merge_epilogue(accs, e_refs, o_ref, i, j):
        ga, gb, ya, yb = accs
        ba_ref, bb_ref = e_refs
        out = _sigmoid(ga + ba_ref[...]) * ya + _sigmoid(gb + bb_ref[...]) * yb
        o_ref[...] = out.astype(o_ref.dtype)

    def residual_epilogue(accs, e_refs, o_ref, i, j):
        o_ref[...] = alpha * e_refs[0][...] + accs[0]

    def relu2_epilogue(accs, e_refs, o_ref, i, j):
        h = jnp.maximum(accs[0], 0.0)
        o_ref[...] = (h * h).astype(o_ref.dtype)

    x32 = x.reshape(m, d_model)
    x16 = x32.astype(BF16)
    bn_merge = 512
    bm_merge = _tile(m, 512)
    gate_a_off = (4 * W_A + 2 * W_B) // bn_merge
    gate_b_off = gate_a_off + d_model // bn_merge

    for l in range(depth):
        pos_map = lambda i, j: (i % pos_blocks, 0)
        qkvg = _fused_matmul(
            "in_proj_qkvg", [x16], [(0, w_in_b, l, 0)],
            [(cos2, (bm, DK_A), pos_map), (sin2, (bm, DK_A), pos_map)],
            qkvg_epilogue, 4 * W_A, BF16, bm, bn, rhs_resident=False)
        xg = _fused_matmul(
            "in_proj_xg", [x16], [(0, w_in_b, l, 4 * W_A // bn)], [],
            xg_epilogue, 2 * W_B, F32, bm, bn, rhs_resident=False)

        ret = _retention(qkvg, dmat, qdec, kdec, ret_gn_w[l].reshape(1, W_A), batch, seq)
        rec = _rglru(xg, conv_w[l], conv_b[l].reshape(1, W_B), rg_wa_b[l], rg_ba[l].reshape(1, W_B),
                     rg_wx_b[l], rg_bx[l].reshape(1, W_B), rg_lambda[l].reshape(1, W_B), batch, seq)

        bg = b_gate[l].reshape(1, 2 * d_model)
        merged = _fused_matmul(
            "merge", [x16, ret, rec],
            [(0, w_in_b, l, gate_a_off), (0, w_in_b, l, gate_b_off),
             (1, w_br_a_b, l, 0), (2, w_br_b_b, l, 0)],
            [(bg, (1, bn_merge), lambda i, j: (0, j)),
             (bg, (1, bn_merge), lambda i, j: (0, d_model // bn_merge + j))],
            merge_epilogue, d_model, BF16, bm_merge, bn_merge, rhs_resident=True)

        y1 = _fused_matmul(
            "out_proj", [merged], [(0, w_o_b, l, 0)],
            [(x32, (bm_merge, bn), lambda i, j: (i, j))],
            residual_epilogue, d_model, F32, bm_merge, bn, rhs_resident=True)
        x32, x16 = _layernorm(y1, ln1_g[l].reshape(1, d_model), ln1_b[l].reshape(1, d_model))

        hid = _fused_matmul(
            "mlp_up", [x16], [(0, w_up_b, l, 0)], [],
            relu2_epilogue, d_ff, BF16, bm, bn, rhs_resident=False)
        y2 = _matmul_residual_ktiled("mlp_down", hid, w_down_b, l, x32, alpha,
                                     bm, bn, _tile(d_ff, 2048))
        x32, x16 = _layernorm(y2, ln2_g[l].reshape(1, d_model), ln2_b[l].reshape(1, d_model))

    return x32.reshape(batch, seq, d_model)
```

```python
import math

import jax
import jax.numpy as jnp
from jax import lax
from jax.experimental import pallas as pl
from jax.experimental.pallas import tpu as pltpu

F32 = jnp.float32
BF16 = jnp.bfloat16

H_A = 8
DK_A = 128
W_A = H_A * DK_A
H_B = 8
BW_B = 128
W_B = H_B * BW_B
CONV_K = 4
C_RG = 8.0
CHUNK = 64
EPS = 1e-5
ROPE_BASE = 10000.0

V7X_VMEM_REQUEST_CAP = 60 * 1024 * 1024
LANES = 128
SUBLANES = 8
BF16_SUBLANES = 16

RET_BLOCK = 256
SCAN_TILE = 256
LN_ROWS = 16
LN_UNROLL = 8
LN_STEPS = 2


def _nbytes(shape, dtype):
    return math.prod(shape) * jnp.dtype(dtype).itemsize


def _params(semantics, block_bytes, scratch_bytes=0, temp_bytes=0):
    need = 2 * block_bytes + scratch_bytes + temp_bytes + 4 * 1024 * 1024
    limit = min(max(need, 32 * 1024 * 1024), V7X_VMEM_REQUEST_CAP)
    return pltpu.CompilerParams(dimension_semantics=semantics, vmem_limit_bytes=limit)


def _tile(dim, pref):
    t = min(dim, pref)
    while dim % t:
        t //= 2
    return t


def _cast_plan(rows, cols, nsteps):
    ncb = 1
    while ncb <= nsteps:
        if cols % (ncb * LANES) == 0 and nsteps % ncb == 0:
            nrb = nsteps // ncb
            if rows % nrb == 0 and (rows // nrb) % BF16_SUBLANES == 0:
                return rows // nrb, cols // ncb
        ncb *= 2
    raise ValueError(f"no cast blocking for {(rows, cols)} in {nsteps} steps")


def _cast_specs(casts, nsteps, step_of):
    in_specs, out_specs, out_shapes, nbytes = [], [], [], 0
    for (w, lead) in casts:
        _, rows, cols = w.shape
        rb, cb = _cast_plan(rows, cols, nsteps)
        ncb = cols // cb

        def blk(*g, ncb=ncb):
            lin = step_of(*g)
            return lin // ncb, lin % ncb

        in_specs.append(pl.BlockSpec((None, rb, cb), lambda *g, lead=lead, blk=blk: (lead,) + blk(*g)))
        out_specs.append(pl.BlockSpec((rb, cb), lambda *g, blk=blk: blk(*g)))
        out_shapes.append(jax.ShapeDtypeStruct((rows, cols), BF16))
        nbytes += _nbytes((rb, cb), F32) + _nbytes((rb, cb), BF16)
    return in_specs, out_specs, out_shapes, nbytes


def _fused_matmul(name, lhs, dots, extras, epilogue, n_out, out_dtype, bm, bn, rhs_resident,
                  n_chunks=1, casts=()):
    m = lhs[0].shape[0]
    assert m % bm == 0 and n_out % bn == 0 and bn % n_chunks == 0
    if rhs_resident:
        grid = (n_out // bn, m // bm)
        ij = lambda g0, g1: (g1, g0)
    else:
        grid = (m // bm, n_out // bn)
        ij = lambda g0, g1: (g0, g1)

    in_specs, args, block_bytes = [], [], 0
    for a in lhs:
        k = a.shape[1]
        in_specs.append(pl.BlockSpec((bm, k), lambda g0, g1: (ij(g0, g1)[0], 0)))
        args.append(a)
        block_bytes += _nbytes((bm, k), a.dtype)
    for (_, w, off) in dots:
        k = w.shape[0]
        in_specs.append(pl.BlockSpec((k, bn), lambda g0, g1, off=off: (0, off + ij(g0, g1)[1])))
        args.append(w)
        block_bytes += _nbytes((k, bn), w.dtype)
    for (e, bshape, imap) in extras:
        in_specs.append(pl.BlockSpec(bshape, lambda g0, g1, imap=imap: imap(*ij(g0, g1))))
        args.append(e)
        block_bytes += _nbytes(bshape, e.dtype)
    block_bytes += _nbytes((bm, bn), out_dtype)
    c_in, c_out, c_shapes, c_bytes = _cast_specs(casts, grid[0] * grid[1],
                                                  lambda g0, g1: g0 * grid[1] + g1)
    in_specs += c_in
    args += [w for (w, _) in casts]
    block_bytes += c_bytes

    n_lhs, n_dots, n_extras, n_casts = len(lhs), len(dots), len(extras), len(casts)
    cw = bn // n_chunks

    def body(*refs):
        a_refs = refs[:n_lhs]
        w_refs = refs[n_lhs:n_lhs + n_dots]
        e_refs = refs[n_lhs + n_dots:n_lhs + n_dots + n_extras]
        ci_refs = refs[n_lhs + n_dots + n_extras:n_lhs + n_dots + n_extras + n_casts]
        o_ref = refs[n_lhs + n_dots + n_extras + n_casts]
        co_refs = refs[n_lhs + n_dots + n_extras + n_casts + 1:]
        for ci, co in zip(ci_refs, co_refs):
            co[...] = ci[...].astype(BF16)
        for c in range(n_chunks):
            cols = slice(c * cw, (c + 1) * cw)
            accs = [jnp.dot(a_refs[d[0]][...], w_ref[:, cols], preferred_element_type=F32)
                    for d, w_ref in zip(dots, w_refs)]
            epilogue(accs, e_refs, o_ref, cols)

    outs = pl.pallas_call(
        body,
        out_shape=[jax.ShapeDtypeStruct((m, n_out), out_dtype)] + c_shapes,
        grid=grid,
        in_specs=in_specs,
        out_specs=[pl.BlockSpec((bm, bn), lambda g0, g1: ij(g0, g1))] + c_out,
        compiler_params=_params(("parallel", "parallel"), block_bytes,
                                temp_bytes=(n_dots + 1) * bm * cw * 4),
        name=name,
    )(*args)
    return outs


def _gelu_tanh(x):
    c = math.sqrt(2.0 / math.pi)
    return 0.5 * x * (1.0 + jnp.tanh(c * (x + 0.044715 * (x * x * x))))


def _sigmoid(x):
    return 1.0 / (1.0 + jnp.exp(-x))


def _matmul_residual_ln(name, a, w, resid, alpha, gamma, beta, bm, bk, casts=()):
    m, kdim = a.shape
    n = w.shape[1]
    nk = kdim // bk
    rc = bm // nk
    ln_rows = bm // LN_STEPS
    assert m % bm == 0 and kdim % bk == 0 and bm % nk == 0 and rc % SUBLANES == 0
    assert ln_rows % (LN_ROWS * LN_UNROLL) == 0
    grid = (m // bm, nk + LN_STEPS)
    n_casts = len(casts)
    n_tiles = m // bm
    tile_of = lambda i, k: jnp.where(k < nk, i, jnp.minimum(i + 1, n_tiles - 1))
    kstep_of = lambda i, k: jnp.where(k < nk, k, 0)
    cast_step_of = lambda i, k: jnp.minimum(i * nk + jnp.minimum(k, nk), n_tiles * nk - 1)
    ee = lambda k: jnp.maximum(k - nk, 0)

    def body(*refs):
        a_ref, w_ref, r_ref, g_ref, b_ref = refs[:5]
        ci_refs = refs[5:5 + n_casts]
        o32_ref, o16_ref = refs[5 + n_casts:7 + n_casts]
        co_refs = refs[7 + n_casts:-1]
        acc_ref = refs[-1]
        k = pl.program_id(1)

        def cast_blocks():
            for ci, co in zip(ci_refs, co_refs):
                co[...] = ci[...].astype(BF16)

        @pl.when(k == 0)
        def _():
            cast_blocks()
            acc_ref[...] = jnp.dot(a_ref[...], w_ref[...], preferred_element_type=F32)
            acc_ref[0:rc, :] += alpha * r_ref[...]

        @pl.when(jnp.logical_and(k > 0, k < nk))
        def _():
            cast_blocks()
            acc_ref[...] += jnp.dot(a_ref[...], w_ref[...], preferred_element_type=F32)
            rows = pl.ds(pl.multiple_of(k * rc, rc), rc)
            acc_ref[rows, :] += alpha * r_ref[...]

        @pl.when(k >= nk)
        def _():
            g = g_ref[...]
            b = b_ref[...]
            base = (k - nk) * ln_rows

            def norm_rows(t, carry):
                src = pl.ds(pl.multiple_of(base + t * LN_ROWS, LN_ROWS), LN_ROWS)
                dst = pl.ds(pl.multiple_of(t * LN_ROWS, LN_ROWS), LN_ROWS)
                y = acc_ref[src, :]
                mu = jnp.mean(y, axis=-1, keepdims=True)
                d = y - mu
                var = jnp.mean(d * d, axis=-1, keepdims=True)
                out = d * lax.rsqrt(var + EPS) * g + b
                o32_ref[dst, :] = out
                o16_ref[dst, :] = out.astype(BF16)
                return carry

            lax.fori_loop(0, ln_rows // LN_ROWS, norm_rows, 0, unroll=LN_UNROLL)

    c_in, c_out, c_shapes, c_bytes = _cast_specs(casts, n_tiles * nk, cast_step_of)
    vec = pl.BlockSpec((1, n), lambda i, k: (0, 0))
    out_rows = pl.BlockSpec((ln_rows, n), lambda i, k: (i * LN_STEPS + ee(k), 0))
    block_bytes = _nbytes((bm, bk), BF16) + _nbytes((bk, n), BF16) + _nbytes((rc, n), F32) \
        + _nbytes((ln_rows, n), F32) + _nbytes((ln_rows, n), BF16) + c_bytes
    return pl.pallas_call(
        body,
        out_shape=[jax.ShapeDtypeStruct((m, n), F32), jax.ShapeDtypeStruct((m, n), BF16)] + c_shapes,
        grid=grid,
        in_specs=[pl.BlockSpec((bm, bk), lambda i, k: (tile_of(i, k), kstep_of(i, k))),
                  pl.BlockSpec((bk, n), lambda i, k: (kstep_of(i, k), 0)),
                  pl.BlockSpec((rc, n), lambda i, k: (tile_of(i, k) * nk + kstep_of(i, k), 0)),
                  vec, vec] + c_in,
        out_specs=[out_rows, out_rows] + c_out,
        scratch_shapes=[pltpu.VMEM((bm, n), F32)],
        compiler_params=_params(("arbitrary", "arbitrary"), block_bytes, scratch_bytes=bm * n * 4,
                                temp_bytes=2 * 1024 * 1024),
        name=name,
    )(a, w, resid, gamma, beta, *[cw for (cw, _) in casts])


def _retention_body(q_ref, k_ref, v_ref, g_ref, d_ref, qd_ref, kd_ref, gain_ref, o_ref):
    seq = q_ref.shape[0]
    blk = d_ref.shape[0]
    dmat = d_ref[...]
    qdec = qd_ref[...]
    kdec = kd_ref[...]
    block_decay = qdec[blk - 1:blk, :]
    gain = gain_ref[...]
    state = jnp.zeros((DK_A, DK_A), F32)
    for n in range(seq // blk):
        rows = pl.ds(n * blk, blk)
        q = q_ref[rows, :]
        k = k_ref[rows, :]
        v = v_ref[rows, :]
        s = lax.dot_general(q, k, (((1,), (1,)), ((), ())), preferred_element_type=F32)
        o = jnp.dot((s * dmat).astype(BF16), v, preferred_element_type=F32)
        q_in = (q.astype(F32) * qdec).astype(BF16)
        o = o + jnp.dot(q_in, state.astype(BF16), preferred_element_type=F32)
        k_out_t = (k.astype(F32) * kdec).T.astype(BF16)
        state = state * block_decay + jnp.dot(k_out_t, v, preferred_element_type=F32)
        mu = jnp.mean(o, axis=-1, keepdims=True)
        d = o - mu
        var = jnp.mean(d * d, axis=-1, keepdims=True)
        y = d * lax.rsqrt(var + EPS) * gain
        o_ref[rows, :] = (g_ref[rows, :].astype(F32) * y).astype(o_ref.dtype)


def _retention(q, k, v, g, dmat, qdec, kdec, gain, batch, seq):
    m = q.shape[0]
    blk = dmat.shape[1]
    seq_head = pl.BlockSpec((seq, DK_A), lambda b, h: (b, h))
    head = lambda b, h: (h, 0, 0)
    block_bytes = 5 * _nbytes((seq, DK_A), BF16) + _nbytes((blk, blk), F32) + 2 * _nbytes((blk, DK_A), F32)
    return pl.pallas_call(
        _retention_body,
        out_shape=jax.ShapeDtypeStruct((m, W_A), BF16),
        grid=(batch, H_A),
        in_specs=[seq_head, seq_head, seq_head, seq_head,
                  pl.BlockSpec((None, blk, blk), head),
                  pl.BlockSpec((None, blk, DK_A), head),
                  pl.BlockSpec((None, blk, DK_A), head),
                  pl.BlockSpec((1, DK_A), lambda b, h: (0, h))],
        out_specs=seq_head,
        compiler_params=_params(("parallel", "parallel"), block_bytes, temp_bytes=8 * blk * blk * 4),
        name="retention",
    )(q, k, v, g, dmat, qdec, kdec, gain)


def _rglru_body(x_ref, gg_ref, cw_ref, cb_ref, wa_ref, ba_ref, wx_ref, bx_ref, lam_ref, o_ref,
                xpad, a_s, u_s, h_s):
    nb, tt, _ = x_ref.shape
    t = pl.program_id(1)

    @pl.when(t == 0)
    def _():
        xpad[:, 0:SUBLANES, :] = jnp.zeros((nb, SUBLANES, BW_B), F32)
        h_s[...] = jnp.zeros_like(h_s)

    xpad[:, SUBLANES:, :] = x_ref[...]
    cw = cw_ref[...]
    xc = cb_ref[...][None, :, :]
    for tap in range(CONV_K):
        lo = SUBLANES - (CONV_K - 1) + tap
        xc = xc + xpad[:, lo:lo + tt, :] * cw[tap:tap + 1, :][None, :, :]
    xpad[:, 0:SUBLANES, :] = xpad[:, tt:tt + SUBLANES, :]

    xc2 = xc.reshape(nb * tt, BW_B)
    xb = xc2.astype(BF16)
    r = _sigmoid(jnp.dot(xb, wa_ref[...].astype(BF16), preferred_element_type=F32) + ba_ref[...])
    ig = _sigmoid(jnp.dot(xb, wx_ref[...].astype(BF16), preferred_element_type=F32) + bx_ref[...])
    neg_lam = -lam_ref[...]
    softplus = jnp.maximum(neg_lam, 0.0) + jnp.log(1.0 + jnp.exp(-jnp.abs(neg_lam)))
    a = jnp.exp((-C_RG * softplus) * r)
    a_s[...] = a
    u_s[...] = jnp.sqrt(1.0 - a * a) * (ig * xc2)

    def step(s, h):
        rows = pl.ds(s, nb, stride=tt)
        h = a_s[rows, :] * h + u_s[rows, :]
        u_s[rows, :] = h
        return h

    h_s[...] = lax.fori_loop(0, tt, step, h_s[...], unroll=8)
    o_ref[...] = (u_s[...].reshape(nb, tt, BW_B) * gg_ref[...]).astype(o_ref.dtype)


def _rglru(xr, gg, conv_w, conv_b, wa, ba, wx, bx, lam, batch, seq):
    tt = _tile(seq, SCAN_TILE)
    xr3 = xr.reshape(batch, seq, W_B)
    gg3 = gg.reshape(batch, seq, W_B)
    tile = pl.BlockSpec((batch, tt, BW_B), lambda h, t: (0, t, h))
    vec = pl.BlockSpec((1, BW_B), lambda h, t: (0, h))
    gate_w = pl.BlockSpec((None, BW_B, BW_B), lambda h, t: (h, 0, 0))
    block_bytes = 2 * _nbytes((batch, tt, BW_B), F32) + _nbytes((batch, tt, BW_B), BF16) \
        + 2 * _nbytes((BW_B, BW_B), F32)
    scratch = [pltpu.VMEM((batch, tt + SUBLANES, BW_B), F32),
               pltpu.VMEM((batch * tt, BW_B), F32),
               pltpu.VMEM((batch * tt, BW_B), F32),
               pltpu.VMEM((batch, BW_B), F32)]
    scratch_bytes = 3 * _nbytes((batch, tt + SUBLANES, BW_B), F32)
    out = pl.pallas_call(
        _rglru_body,
        out_shape=jax.ShapeDtypeStruct((batch, seq, W_B), BF16),
        grid=(H_B, seq // tt),
        in_specs=[tile, tile,
                  pl.BlockSpec((CONV_K, BW_B), lambda h, t: (0, h)),
                  vec, gate_w, vec, gate_w, vec, vec],
        out_specs=tile,
        scratch_shapes=scratch,
        compiler_params=_params(("parallel", "arbitrary"), block_bytes, scratch_bytes,
                                temp_bytes=8 * batch * tt * BW_B * 4),
        name="rglru",
    )(xr3, gg3, conv_w, conv_b, wa, ba, wx, bx, lam)
    return out.reshape(batch * seq, W_B)


def _retention_tables(blk):
    lg = jnp.log1p(-jnp.exp2(-5.0 - jnp.arange(H_A, dtype=F32)))
    idx = jnp.arange(blk, dtype=F32)
    diff = jnp.abs(idx[:, None] - idx[None, :])
    chunk = jnp.arange(blk) // CHUNK
    visible = chunk[None, :] <= chunk[:, None]
    dmat = jnp.where(visible[None], jnp.exp(lg[:, None, None] * diff[None]), 0.0)
    qdec = jnp.exp(lg[:, None] * (idx + 1.0))
    kdec = jnp.exp(lg[:, None] * (blk - 1.0 - idx))
    bcast = lambda t: jnp.broadcast_to(t[:, :, None], (H_A, blk, DK_A))
    return dmat.astype(F32), bcast(qdec), bcast(kdec)


def _rotary_tables(seq):
    pos = jnp.arange(seq, dtype=F32)
    inv_freq = ROPE_BASE ** (-jnp.arange(0, DK_A, 2, dtype=F32) / DK_A)
    ang = pos[:, None] * inv_freq[None, :]
    cos, sin = jnp.cos(ang), jnp.sin(ang)
    return jnp.concatenate([cos, cos], axis=-1), jnp.concatenate([-sin, sin], axis=-1)


def kernel(x, w_in, ret_gn_w, conv_w, conv_b, rg_wa, rg_ba, rg_wx, rg_bx, rg_lambda, w_br_a, w_br_b,
           b_gate, w_o, ln1_g, ln1_b, w_up, w_down, ln2_g, ln2_b):
    batch, seq, d_model = x.shape
    depth = w_in.shape[0]
    d_ff = w_up.shape[-1]
    m = batch * seq
    alpha = (2.0 * depth) ** 0.25
    assert w_in.shape[-1] == 4 * W_A + 2 * W_B + 2 * d_model
    assert batch == SUBLANES, "the scan packs the batch onto the sublanes of one vreg"

    blk = _tile(seq, RET_BLOCK)
    dmat, qdec, kdec = _retention_tables(blk)
    cos2, sin2 = _rotary_tables(seq)

    bm = _tile(seq, 1024)
    bn = 1024
    assert W_A == bn and W_B == bn
    pos_blocks = seq // bm
    pos_map = lambda i, j: (i % pos_blocks, 0)
    k_scale = DK_A ** -0.5

    def rotary_epilogue(scale):
        def epilogue(accs, e_refs, o_ref, cols):
            cos_ref, sin_ref = e_refs
            c, s = cos_ref[...], sin_ref[...]
            acc = accs[0]
            for hh in range(acc.shape[1] // DK_A):
                t = acc[:, hh * DK_A:(hh + 1) * DK_A]
                rot = t * c + pltpu.roll(t, DK_A // 2, 1) * s
                if scale is not None:
                    rot = rot * scale
                lo = cols.start + hh * DK_A
                o_ref[:, lo:lo + DK_A] = rot.astype(o_ref.dtype)
        return epilogue

    def cast_epilogue(accs, e_refs, o_ref, cols):
        o_ref[:, cols] = accs[0].astype(o_ref.dtype)

    def silu_epilogue(accs, e_refs, o_ref, cols):
        o_ref[:, cols] = (accs[0] * _sigmoid(accs[0])).astype(o_ref.dtype)

    def gelu_epilogue(accs, e_refs, o_ref, cols):
        o_ref[:, cols] = _gelu_tanh(accs[0])

    def merge_epilogue(accs, e_refs, o_ref, cols):
        ga, gb, ya, yb = accs
        ba_ref, bb_ref = e_refs
        out = _sigmoid(ga + ba_ref[:, cols]) * ya + _sigmoid(gb + bb_ref[:, cols]) * yb
        o_ref[:, cols] = out.astype(o_ref.dtype)

    def relu2_epilogue(accs, e_refs, o_ref, cols):
        h = jnp.maximum(accs[0], 0.0)
        o_ref[:, cols] = (h * h).astype(o_ref.dtype)

    x32 = x.reshape(m, d_model)
    x16 = x32.astype(BF16)
    w_in_b = w_in[0].astype(BF16)
    bn_merge = 512
    bm_merge = _tile(m, 512)
    gate_a_off = (4 * W_A + 2 * W_B) // bn_merge
    gate_b_off = gate_a_off + d_model // bn_merge
    rot_extras = [(cos2, (bm, DK_A), pos_map), (sin2, (bm, DK_A), pos_map)]

    def in_proj(name, col_block, extras, epilogue, dtype, casts=()):
        return _fused_matmul(name, [x16], [(0, w_in_b, col_block)], extras, epilogue, bn, dtype,
                             bm, bn, rhs_resident=True, n_chunks=2, casts=casts)

    for l in range(depth):
        q, w_br_a_b = in_proj("in_proj_q", 0, rot_extras, rotary_epilogue(None), BF16, [(w_br_a, l)])
        k, w_br_b_b = in_proj("in_proj_k", 1, rot_extras, rotary_epilogue(k_scale), BF16, [(w_br_b, l)])
        v, = in_proj("in_proj_v", 2, [], cast_epilogue, BF16)
        g, = in_proj("in_proj_g", 3, [], silu_epilogue, BF16)
        xr, = in_proj("in_proj_xr", 4, [], cast_epilogue, F32)
        gg, = in_proj("in_proj_gr", 5, [], gelu_epilogue, F32)

        ret = _retention(q, k, v, g, dmat, qdec, kdec, ret_gn_w[l].reshape(1, W_A), batch, seq)
        rec = _rglru(xr, gg, conv_w[l], conv_b[l].reshape(1, W_B), rg_wa[l], rg_ba[l].reshape(1, W_B),
                     rg_wx[l], rg_bx[l].reshape(1, W_B), rg_lambda[l].reshape(1, W_B), batch, seq)

        bg = b_gate[l].reshape(1, 2 * d_model)
        merged, w_up_b, w_o_b = _fused_matmul(
            "merge", [x16, ret, rec],
            [(0, w_in_b, gate_a_off), (0, w_in_b, gate_b_off), (1, w_br_a_b, 0), (2, w_br_b_b, 0)],
            [(bg, (1, bn_merge), lambda i, j: (0, j)),
             (bg, (1, bn_merge), lambda i, j: (0, d_model // bn_merge + j))],
            merge_epilogue, d_model, BF16, bm_merge, bn_merge, rhs_resident=True,
            casts=[(w_up, l), (w_o, l)])

        x32, x16 = _matmul_residual_ln(
            "out_proj_ln", merged, w_o_b, x32, alpha, ln1_g[l].reshape(1, d_model),
            ln1_b[l].reshape(1, d_model), bm_merge, _tile(d_model, 1024))

        hid, w_down_b = _fused_matmul(
            "mlp_up", [x16], [(0, w_up_b, 0)], [], relu2_epilogue, d_ff, BF16, bm, bn,
            rhs_resident=False, n_chunks=2, casts=[(w_down, l)])

        next_casts = [(w_in, l + 1)] if l + 1 < depth else []
        outs = _matmul_residual_ln(
            "mlp_down_ln", hid, w_down_b, x32, alpha, ln2_g[l].reshape(1, d_model),
            ln2_b[l].reshape(1, d_model), bm_merge, _tile(d_ff, 1024), casts=next_casts)
        x32, x16 = outs[0], outs[1]
        if next_casts:
            w_in_b = outs[2]

    return x32.reshape(batch, seq, d_model)
```

```python
import math

import jax
import jax.numpy as jnp
from jax import lax
from jax.experimental import pallas as pl
from jax.experimental.pallas import tpu as pltpu

F32 = jnp.float32
BF16 = jnp.bfloat16

H_A = 8
DK_A = 128
W_A = H_A * DK_A
H_B = 8
BW_B = 128
W_B = H_B * BW_B
CONV_K = 4
C_RG = 8.0
CHUNK = 64
EPS = 1e-5
ROPE_BASE = 10000.0

V7X_VMEM_REQUEST_CAP = 60 * 1024 * 1024
LANES = 128
SUBLANES = 8
BF16_SUBLANES = 16

RET_BLOCK = 256
SCAN_TILE = 256
LN_ROWS = 16
LN_UNROLL = 8
LN_STEPS = 8


def _nbytes(shape, dtype):
    return math.prod(shape) * jnp.dtype(dtype).itemsize


def _params(semantics, block_bytes, scratch_bytes=0, temp_bytes=0):
    need = 2 * block_bytes + scratch_bytes + temp_bytes + 4 * 1024 * 1024
    limit = min(max(need, 32 * 1024 * 1024), V7X_VMEM_REQUEST_CAP)
    return pltpu.CompilerParams(dimension_semantics=semantics, vmem_limit_bytes=limit)


def _tile(dim, pref):
    t = min(dim, pref)
    while dim % t:
        t //= 2
    return t


def _cast_plan(rows, cols, nsteps):
    ncb = 1
    while ncb <= nsteps:
        if cols % (ncb * LANES) == 0 and nsteps % ncb == 0:
            nrb = nsteps // ncb
            if rows % nrb == 0 and (rows // nrb) % BF16_SUBLANES == 0:
                return rows // nrb, cols // ncb
        ncb *= 2
    raise ValueError(f"no cast blocking for {(rows, cols)} in {nsteps} steps")


def _cast_specs(casts, nsteps, step_of):
    in_specs, out_specs, out_shapes, nbytes = [], [], [], 0
    for (w, lead) in casts:
        _, rows, cols = w.shape
        rb, cb = _cast_plan(rows, cols, nsteps)
        ncb = cols // cb

        def blk(*g, ncb=ncb):
            lin = step_of(*g)
            return lin // ncb, lin % ncb

        in_specs.append(pl.BlockSpec((None, rb, cb), lambda *g, lead=lead, blk=blk: (lead,) + blk(*g)))
        out_specs.append(pl.BlockSpec((rb, cb), lambda *g, blk=blk: blk(*g)))
        out_shapes.append(jax.ShapeDtypeStruct((rows, cols), BF16))
        nbytes += _nbytes((rb, cb), F32) + _nbytes((rb, cb), BF16)
    return in_specs, out_specs, out_shapes, nbytes


def _fused_matmul(name, lhs, dots, extras, epilogue, n_out, out_dtype, bm, bn, rhs_resident,
                  n_chunks=1, casts=()):
    m = lhs[0].shape[0]
    assert m % bm == 0 and n_out % bn == 0 and bn % n_chunks == 0
    if rhs_resident:
        grid = (n_out // bn, m // bm)
        ij = lambda g0, g1: (g1, g0)
    else:
        grid = (m // bm, n_out // bn)
        ij = lambda g0, g1: (g0, g1)

    in_specs, args, block_bytes = [], [], 0
    for a in lhs:
        k = a.shape[1]
        in_specs.append(pl.BlockSpec((bm, k), lambda g0, g1: (ij(g0, g1)[0], 0)))
        args.append(a)
        block_bytes += _nbytes((bm, k), a.dtype)
    for (_, w, off) in dots:
        k = w.shape[0]
        in_specs.append(pl.BlockSpec((k, bn), lambda g0, g1, off=off: (0, off + ij(g0, g1)[1])))
        args.append(w)
        block_bytes += _nbytes((k, bn), w.dtype)
    for (e, bshape, imap) in extras:
        in_specs.append(pl.BlockSpec(bshape, lambda g0, g1, imap=imap: imap(*ij(g0, g1))))
        args.append(e)
        block_bytes += _nbytes(bshape, e.dtype)
    block_bytes += _nbytes((bm, bn), out_dtype)
    c_in, c_out, c_shapes, c_bytes = _cast_specs(casts, grid[0] * grid[1],
                                                  lambda g0, g1: g0 * grid[1] + g1)
    in_specs += c_in
    args += [w for (w, _) in casts]
    block_bytes += c_bytes

    n_lhs, n_dots, n_extras, n_casts = len(lhs), len(dots), len(extras), len(casts)
    cw = bn // n_chunks

    def body(*refs):
        a_refs = refs[:n_lhs]
        w_refs = refs[n_lhs:n_lhs + n_dots]
        e_refs = refs[n_lhs + n_dots:n_lhs + n_dots + n_extras]
        ci_refs = refs[n_lhs + n_dots + n_extras:n_lhs + n_dots + n_extras + n_casts]
        o_ref = refs[n_lhs + n_dots + n_extras + n_casts]
        co_refs = refs[n_lhs + n_dots + n_extras + n_casts + 1:]
        for ci, co in zip(ci_refs, co_refs):
            co[...] = ci[...].astype(BF16)
        for c in range(n_chunks):
            cols = slice(c * cw, (c + 1) * cw)
            accs = [jnp.dot(a_refs[d[0]][...], w_ref[:, cols], preferred_element_type=F32)
                    for d, w_ref in zip(dots, w_refs)]
            epilogue(accs, e_refs, o_ref, cols)

    outs = pl.pallas_call(
        body,
        out_shape=[jax.ShapeDtypeStruct((m, n_out), out_dtype)] + c_shapes,
        grid=grid,
        in_specs=in_specs,
        out_specs=[pl.BlockSpec((bm, bn), lambda g0, g1: ij(g0, g1))] + c_out,
        compiler_params=_params(("parallel", "parallel"), block_bytes,
                                temp_bytes=(n_dots + 1) * bm * cw * 4),
        name=name,
    )(*args)
    return outs


def _gelu_tanh(x):
    c = math.sqrt(2.0 / math.pi)
    return 0.5 * x * (1.0 + jnp.tanh(c * (x + 0.044715 * (x * x * x))))


def _sigmoid(x):
    return 0.5 * jnp.tanh(0.5 * x) + 0.5


def _matmul_residual_ln(name, a, w, resid, alpha, gamma, beta, bm, bk, casts=()):
    m, kdim = a.shape
    n = w.shape[1]
    nk = kdim // bk
    ln_rows = bm // LN_STEPS
    assert m % bm == 0 and kdim % bk == 0
    assert ln_rows % (LN_ROWS * LN_UNROLL) == 0
    grid = (m // bm, nk + LN_STEPS)
    n_casts = len(casts)
    n_tiles = m // bm
    tile_of = lambda i, k: jnp.where(k < nk, i, jnp.minimum(i + 1, n_tiles - 1))
    kstep_of = lambda i, k: jnp.where(k < nk, k, 0)
    cast_step_of = lambda i, k: jnp.minimum(i * nk + jnp.minimum(k, nk), n_tiles * nk - 1)
    ee = lambda k: jnp.maximum(k - nk, 0)

    def body(*refs):
        a_ref, w_ref, r_ref, g_ref, b_ref = refs[:5]
        ci_refs = refs[5:5 + n_casts]
        o32_ref, o16_ref = refs[5 + n_casts:7 + n_casts]
        co_refs = refs[7 + n_casts:-1]
        acc_ref = refs[-1]
        k = pl.program_id(1)

        def cast_blocks():
            for ci, co in zip(ci_refs, co_refs):
                co[...] = ci[...].astype(BF16)

        @pl.when(k == 0)
        def _():
            cast_blocks()
            acc_ref[...] = jnp.dot(a_ref[...], w_ref[...], preferred_element_type=F32)

        @pl.when(jnp.logical_and(k > 0, k < nk))
        def _():
            cast_blocks()
            acc_ref[...] += jnp.dot(a_ref[...], w_ref[...], preferred_element_type=F32)

        @pl.when(k >= nk)
        def _():
            g = g_ref[...]
            b = b_ref[...]
            base = (k - nk) * ln_rows

            def norm_rows(t, carry):
                src = pl.ds(pl.multiple_of(base + t * LN_ROWS, LN_ROWS), LN_ROWS)
                dst = pl.ds(pl.multiple_of(t * LN_ROWS, LN_ROWS), LN_ROWS)
                y = alpha * r_ref[dst, :] + acc_ref[src, :]
                mu = jnp.mean(y, axis=-1, keepdims=True)
                d = y - mu
                var = jnp.mean(d * d, axis=-1, keepdims=True)
                out = d * lax.rsqrt(var + EPS) * g + b
                o32_ref[dst, :] = out
                o16_ref[dst, :] = out.astype(BF16)
                return carry

            lax.fori_loop(0, ln_rows // LN_ROWS, norm_rows, 0, unroll=LN_UNROLL)

    c_in, c_out, c_shapes, c_bytes = _cast_specs(casts, n_tiles * nk, cast_step_of)
    vec = pl.BlockSpec((1, n), lambda i, k: (0, 0))
    out_rows = pl.BlockSpec((ln_rows, n), lambda i, k: (i * LN_STEPS + ee(k), 0))
    block_bytes = _nbytes((bm, bk), BF16) + _nbytes((bk, n), BF16) \
        + 2 * _nbytes((ln_rows, n), F32) + _nbytes((ln_rows, n), BF16) + c_bytes
    return pl.pallas_call(
        body,
        out_shape=[jax.ShapeDtypeStruct((m, n), F32), jax.ShapeDtypeStruct((m, n), BF16)] + c_shapes,
        grid=grid,
        in_specs=[pl.BlockSpec((bm, bk), lambda i, k: (tile_of(i, k), kstep_of(i, k))),
                  pl.BlockSpec((bk, n), lambda i, k: (kstep_of(i, k), 0)),
                  out_rows, vec, vec] + c_in,
        out_specs=[out_rows, out_rows] + c_out,
        scratch_shapes=[pltpu.VMEM((bm, n), F32)],
        compiler_params=_params(("arbitrary", "arbitrary"), block_bytes, scratch_bytes=bm * n * 4,
                                temp_bytes=2 * 1024 * 1024),
        name=name,
    )(a, w, resid, gamma, beta, *[cw for (cw, _) in casts])


def _retention_body(q_ref, k_ref, v_ref, g_ref, d_ref, qd_ref, kd_ref, gain_ref, o_ref):
    seq = q_ref.shape[0]
    blk = d_ref.shape[0]
    dmat = d_ref[...]
    qdec = qd_ref[...]
    kdec = kd_ref[...]
    block_decay = qdec[blk - 1:blk, :]
    gain = gain_ref[...]
    state = jnp.zeros((DK_A, DK_A), F32)
    for n in range(seq // blk):
        rows = pl.ds(n * blk, blk)
        q = q_ref[rows, :]
        k = k_ref[rows, :]
        v = v_ref[rows, :]
        s = lax.dot_general(q, k, (((1,), (1,)), ((), ())), preferred_element_type=F32)
        o = jnp.dot((s * dmat).astype(BF16), v, preferred_element_type=F32)
        q_in = (q.astype(F32) * qdec).astype(BF16)
        o = o + jnp.dot(q_in, state.astype(BF16), preferred_element_type=F32)
        k_out_t = (k.astype(F32) * kdec).T.astype(BF16)
        state = state * block_decay + jnp.dot(k_out_t, v, preferred_element_type=F32)
        mu = jnp.mean(o, axis=-1, keepdims=True)
        d = o - mu
        var = jnp.mean(d * d, axis=-1, keepdims=True)
        y = d * lax.rsqrt(var + EPS) * gain
        o_ref[rows, :] = (g_ref[rows, :].astype(F32) * y).astype(o_ref.dtype)


def _retention(q, k, v, g, dmat, qdec, kdec, gain, batch, seq):
    m = q.shape[0]
    blk = dmat.shape[1]
    seq_head = pl.BlockSpec((seq, DK_A), lambda b, h: (b, h))
    head = lambda b, h: (h, 0, 0)
    block_bytes = 5 * _nbytes((seq, DK_A), BF16) + _nbytes((blk, blk), F32) + 2 * _nbytes((blk, DK_A), F32)
    return pl.pallas_call(
        _retention_body,
        out_shape=jax.ShapeDtypeStruct((m, W_A), BF16),
        grid=(batch, H_A),
        in_specs=[seq_head, seq_head, seq_head, seq_head,
                  pl.BlockSpec((None, blk, blk), head),
                  pl.BlockSpec((None, blk, DK_A), head),
                  pl.BlockSpec((None, blk, DK_A), head),
                  pl.BlockSpec((1, DK_A), lambda b, h: (0, h))],
        out_specs=seq_head,
        compiler_params=_params(("parallel", "parallel"), block_bytes, temp_bytes=8 * blk * blk * 4),
        name="retention",
    )(q, k, v, g, dmat, qdec, kdec, gain)


def _rglru_body(x_ref, gg_ref, cw_ref, cb_ref, wa_ref, ba_ref, wx_ref, bx_ref, lam_ref, o_ref,
                xpad, a_s, u_s, h_s):
    nb, tt, _ = x_ref.shape
    halo = (CONV_K - 1) * nb
    t = pl.program_id(1)

    @pl.when(t == 0)
    def _():
        xpad[0:halo, :] = jnp.zeros((halo, BW_B), F32)
        h_s[...] = jnp.zeros_like(h_s)

    xpad[halo:, :] = jnp.swapaxes(x_ref[...], 0, 1).reshape(tt * nb, BW_B)
    cw = cw_ref[...]
    xc2 = cb_ref[...]
    for tap in range(CONV_K):
        xc2 = xc2 + xpad[tap * nb:(tap + tt) * nb, :] * cw[tap:tap + 1, :]
    xpad[0:halo, :] = xpad[tt * nb:tt * nb + halo, :]

    xb = xc2.astype(BF16)
    r = _sigmoid(jnp.dot(xb, wa_ref[...].astype(BF16), preferred_element_type=F32) + ba_ref[...])
    ig = _sigmoid(jnp.dot(xb, wx_ref[...].astype(BF16), preferred_element_type=F32) + bx_ref[...])
    neg_lam = -lam_ref[...]
    softplus = jnp.maximum(neg_lam, 0.0) + jnp.log(1.0 + jnp.exp(-jnp.abs(neg_lam)))
    a = jnp.exp((-C_RG * softplus) * r)
    a_s[...] = a
    u_s[...] = jnp.sqrt(1.0 - a * a) * (ig * xc2)

    def step(s, h):
        rows = pl.ds(pl.multiple_of(s * nb, nb), nb)
        h = a_s[rows, :] * h + u_s[rows, :]
        u_s[rows, :] = h
        return h

    h_s[...] = lax.fori_loop(0, tt, step, h_s[...], unroll=8)
    hs = jnp.swapaxes(u_s[...].reshape(tt, nb, BW_B), 0, 1)
    o_ref[...] = (hs * gg_ref[...]).astype(o_ref.dtype)


def _rglru(xr, gg, conv_w, conv_b, wa, ba, wx, bx, lam, batch, seq):
    tt = _tile(seq, SCAN_TILE)
    xr3 = xr.reshape(batch, seq, W_B)
    gg3 = gg.reshape(batch, seq, W_B)
    tile = pl.BlockSpec((batch, tt, BW_B), lambda h, t: (0, t, h))
    vec = pl.BlockSpec((1, BW_B), lambda h, t: (0, h))
    gate_w = pl.BlockSpec((None, BW_B, BW_B), lambda h, t: (h, 0, 0))
    block_bytes = 2 * _nbytes((batch, tt, BW_B), F32) + _nbytes((batch, tt, BW_B), BF16) \
        + 2 * _nbytes((BW_B, BW_B), F32)
    scratch = [pltpu.VMEM(((tt + CONV_K - 1) * batch, BW_B), F32),
               pltpu.VMEM((batch * tt, BW_B), F32),
               pltpu.VMEM((batch * tt, BW_B), F32),
               pltpu.VMEM((batch, BW_B), F32)]
    scratch_bytes = 3 * _nbytes((batch, tt + SUBLANES, BW_B), F32)
    out = pl.pallas_call(
        _rglru_body,
        out_shape=jax.ShapeDtypeStruct((batch, seq, W_B), BF16),
        grid=(H_B, seq // tt),
        in_specs=[tile, tile,
                  pl.BlockSpec((CONV_K, BW_B), lambda h, t: (0, h)),
                  vec, gate_w, vec, gate_w, vec, vec],
        out_specs=tile,
        scratch_shapes=scratch,
        compiler_params=_params(("parallel", "arbitrary"), block_bytes, scratch_bytes,
                                temp_bytes=8 * batch * tt * BW_B * 4),
        name="rglru",
    )(xr3, gg3, conv_w, conv_b, wa, ba, wx, bx, lam)
    return out.reshape(batch * seq, W_B)


def _retention_tables(blk):
    lg = jnp.log1p(-jnp.exp2(-5.0 - jnp.arange(H_A, dtype=F32)))
    idx = jnp.arange(blk, dtype=F32)
    diff = jnp.abs(idx[:, None] - idx[None, :])
    chunk = jnp.arange(blk) // CHUNK
    visible = chunk[None, :] <= chunk[:, None]
    dmat = jnp.where(visible[None], jnp.exp(lg[:, None, None] * diff[None]), 0.0)
    qdec = jnp.exp(lg[:, None] * (idx + 1.0))
    kdec = jnp.exp(lg[:, None] * (blk - 1.0 - idx))
    bcast = lambda t: jnp.broadcast_to(t[:, :, None], (H_A, blk, DK_A))
    return dmat.astype(F32), bcast(qdec), bcast(kdec)


def _rotary_tables(seq):
    pos = jnp.arange(seq, dtype=F32)
    inv_freq = ROPE_BASE ** (-jnp.arange(0, DK_A, 2, dtype=F32) / DK_A)
    ang = pos[:, None] * inv_freq[None, :]
    cos, sin = jnp.cos(ang), jnp.sin(ang)
    return jnp.concatenate([cos, cos], axis=-1), jnp.concatenate([-sin, sin], axis=-1)


def kernel(x, w_in, ret_gn_w, conv_w, conv_b, rg_wa, rg_ba, rg_wx, rg_bx, rg_lambda, w_br_a, w_br_b,
           b_gate, w_o, ln1_g, ln1_b, w_up, w_down, ln2_g, ln2_b):
    batch, seq, d_model = x.shape
    depth = w_in.shape[0]
    d_ff = w_up.shape[-1]
    m = batch * seq
    alpha = (2.0 * depth) ** 0.25
    assert w_in.shape[-1] == 4 * W_A + 2 * W_B + 2 * d_model
    assert batch == SUBLANES, "the scan packs the batch onto the sublanes of one vreg"

    blk = _tile(seq, RET_BLOCK)
    dmat, qdec, kdec = _retention_tables(blk)
    cos2, sin2 = _rotary_tables(seq)

    bm = _tile(seq, 1024)
    bn = 1024
    assert W_A == bn and W_B == bn
    pos_blocks = seq // bm
    pos_map = lambda i, j: (i % pos_blocks, 0)
    k_scale = DK_A ** -0.5

    def rotary_epilogue(scale):
        def epilogue(accs, e_refs, o_ref, cols):
            cos_ref, sin_ref = e_refs
            c, s = cos_ref[...], sin_ref[...]
            acc = accs[0]
            for hh in range(acc.shape[1] // DK_A):
                t = acc[:, hh * DK_A:(hh + 1) * DK_A]
                rot = t * c + pltpu.roll(t, DK_A // 2, 1) * s
                if scale is not None:
                    rot = rot * scale
                lo = cols.start + hh * DK_A
                o_ref[:, lo:lo + DK_A] = rot.astype(o_ref.dtype)
        return epilogue

    def cast_epilogue(accs, e_refs, o_ref, cols):
        o_ref[:, cols] = accs[0].astype(o_ref.dtype)

    def silu_epilogue(accs, e_refs, o_ref, cols):
        o_ref[:, cols] = (accs[0] * _sigmoid(accs[0])).astype(o_ref.dtype)

    def gelu_epilogue(accs, e_refs, o_ref, cols):
        o_ref[:, cols] = _gelu_tanh(accs[0])

    def merge_epilogue(accs, e_refs, o_ref, cols):
        ga, gb, ya, yb = accs
        ba_ref, bb_ref = e_refs
        out = _sigmoid(ga + ba_ref[:, cols]) * ya + _sigmoid(gb + bb_ref[:, cols]) * yb
        o_ref[:, cols] = out.astype(o_ref.dtype)

    def relu2_epilogue(accs, e_refs, o_ref, cols):
        h = jnp.maximum(accs[0], 0.0)
        o_ref[:, cols] = (h * h).astype(o_ref.dtype)

    x32 = x.reshape(m, d_model)
    x16 = x32.astype(BF16)
    w_in_b = w_in[0].astype(BF16)
    bn_merge = 512
    bm_merge = _tile(m, 512)
    bm_ln = _tile(m, 1024)
    gate_a_off = (4 * W_A + 2 * W_B) // bn_merge
    gate_b_off = gate_a_off + d_model // bn_merge
    rot_extras = [(cos2, (bm, DK_A), pos_map), (sin2, (bm, DK_A), pos_map)]

    def in_proj(name, col_block, extras, epilogue, dtype, casts=()):
        return _fused_matmul(name, [x16], [(0, w_in_b, col_block)], extras, epilogue, bn, dtype,
                             bm, bn, rhs_resident=True, n_chunks=2, casts=casts)

    for l in range(depth):
        q, w_br_a_b = in_proj("in_proj_q", 0, rot_extras, rotary_epilogue(None), BF16, [(w_br_a, l)])
        k, w_br_b_b = in_proj("in_proj_k", 1, rot_extras, rotary_epilogue(k_scale), BF16, [(w_br_b, l)])
        v, = in_proj("in_proj_v", 2, [], cast_epilogue, BF16)
        g, = in_proj("in_proj_g", 3, [], silu_epilogue, BF16)
        xr, = in_proj("in_proj_xr", 4, [], cast_epilogue, F32)
        gg, = in_proj("in_proj_gr", 5, [], gelu_epilogue, F32)

        ret = _retention(q, k, v, g, dmat, qdec, kdec, ret_gn_w[l].reshape(1, W_A), batch, seq)
        rec = _rglru(xr, gg, conv_w[l], conv_b[l].reshape(1, W_B), rg_wa[l], rg_ba[l].reshape(1, W_B),
                     rg_wx[l], rg_bx[l].reshape(1, W_B), rg_lambda[l].reshape(1, W_B), batch, seq)

        bg = b_gate[l].reshape(1, 2 * d_model)
        merged, w_up_b, w_o_b = _fused_matmul(
            "merge", [x16, ret, rec],
            [(0, w_in_b, gate_a_off), (0, w_in_b, gate_b_off), (1, w_br_a_b, 0), (2, w_br_b_b, 0)],
            [(bg, (1, bn_merge), lambda i, j: (0, j)),
             (bg, (1, bn_merge), lambda i, j: (0, d_model // bn_merge + j))],
            merge_epilogue, d_model, BF16, bm_merge, bn_merge, rhs_resident=True, n_chunks=2,
            casts=[(w_up, l), (w_o, l)])

        x32, x16 = _matmul_residual_ln(
            "out_proj_ln", merged, w_o_b, x32, alpha, ln1_g[l].reshape(1, d_model),
            ln1_b[l].reshape(1, d_model), bm_ln, _tile(d_model, 1024))

        hid, w_down_b = _fused_matmul(
            "mlp_up", [x16], [(0, w_up_b, 0)], [], relu2_epilogue, d_ff, BF16, bm, bn,
            rhs_resident=False, n_chunks=2, casts=[(w_down, l)])

        next_casts = [(w_in, l + 1)] if l + 1 < depth else []
        outs = _matmul_residual_ln(
            "mlp_down_ln", hid, w_down_b, x32, alpha, ln2_g[l].reshape(1, d_model),
            ln2_b[l].reshape(1, d_model), bm_ln, _tile(d_ff, 1024), casts=next_casts)
        x32, x16 = outs[0], outs[1]
        if next_casts:
            w_in_b = outs[2]

    return x32.reshape(batch, seq, d_model)
```

```python
import math

import jax
import jax.numpy as jnp
from jax import lax
from jax.experimental import pallas as pl
from jax.experimental.pallas import tpu as pltpu

F32 = jnp.float32
BF16 = jnp.bfloat16

H_A = 8
DK_A = 128
W_A = H_A * DK_A
H_B = 8
BW_B = 128
W_B = H_B * BW_B
CONV_K = 4
C_RG = 8.0
CHUNK = 64
EPS = 1e-5
ROPE_BASE = 10000.0

V7X_VMEM_REQUEST_CAP = 60 * 1024 * 1024
LANES = 128
SUBLANES = 8
BF16_SUBLANES = 16

RET_BLOCK = 256
SCAN_TILE = 256
LN_ROWS = 16
LN_STEPS = 8


def _nbytes(shape, dtype):
    return math.prod(shape) * jnp.dtype(dtype).itemsize


def _params(semantics, block_bytes, scratch_bytes=0, temp_bytes=0):
    need = 2 * block_bytes + scratch_bytes + temp_bytes + 4 * 1024 * 1024
    limit = min(max(need, 32 * 1024 * 1024), V7X_VMEM_REQUEST_CAP)
    return pltpu.CompilerParams(dimension_semantics=semantics, vmem_limit_bytes=limit)


def _tile(dim, pref):
    t = min(dim, pref)
    while dim % t:
        t //= 2
    return t


def _cast_plan(rows, cols, nsteps):
    ncb = 1
    while ncb <= nsteps:
        if cols % (ncb * LANES) == 0 and nsteps % ncb == 0:
            nrb = nsteps // ncb
            if rows % nrb == 0 and (rows // nrb) % BF16_SUBLANES == 0:
                return rows // nrb, cols // ncb
        ncb *= 2
    raise ValueError(f"no cast blocking for {(rows, cols)} in {nsteps} steps")


def _cast_specs(casts, nsteps, step_of):
    in_specs, out_specs, out_shapes, nbytes = [], [], [], 0
    for (w, lead) in casts:
        _, rows, cols = w.shape
        rb, cb = _cast_plan(rows, cols, nsteps)
        ncb = cols // cb

        def blk(*g, ncb=ncb):
            lin = step_of(*g)
            return lin // ncb, lin % ncb

        in_specs.append(pl.BlockSpec((None, rb, cb), lambda *g, lead=lead, blk=blk: (lead,) + blk(*g)))
        out_specs.append(pl.BlockSpec((rb, cb), lambda *g, blk=blk: blk(*g)))
        out_shapes.append(jax.ShapeDtypeStruct((rows, cols), BF16))
        nbytes += _nbytes((rb, cb), F32) + _nbytes((rb, cb), BF16)
    return in_specs, out_specs, out_shapes, nbytes


def _fused_matmul(name, lhs, dots, extras, epilogue, n_out, out_dtype, bm, bn, rhs_resident,
                  n_chunks=1, casts=()):
    m = lhs[0].shape[0]
    assert m % bm == 0 and n_out % bn == 0 and bn % n_chunks == 0
    if rhs_resident:
        grid = (n_out // bn, m // bm)
        ij = lambda g0, g1: (g1, g0)
    else:
        grid = (m // bm, n_out // bn)
        ij = lambda g0, g1: (g0, g1)

    in_specs, args, block_bytes = [], [], 0
    for a in lhs:
        k = a.shape[1]
        in_specs.append(pl.BlockSpec((bm, k), lambda g0, g1: (ij(g0, g1)[0], 0)))
        args.append(a)
        block_bytes += _nbytes((bm, k), a.dtype)
    for (_, w, off) in dots:
        k = w.shape[0]
        in_specs.append(pl.BlockSpec((k, bn), lambda g0, g1, off=off: (0, off + ij(g0, g1)[1])))
        args.append(w)
        block_bytes += _nbytes((k, bn), w.dtype)
    for (e, bshape, imap) in extras:
        in_specs.append(pl.BlockSpec(bshape, lambda g0, g1, imap=imap: imap(*ij(g0, g1))))
        args.append(e)
        block_bytes += _nbytes(bshape, e.dtype)
    block_bytes += _nbytes((bm, bn), out_dtype)
    c_in, c_out, c_shapes, c_bytes = _cast_specs(casts, grid[0] * grid[1],
                                                  lambda g0, g1: g0 * grid[1] + g1)
    in_specs += c_in
    args += [w for (w, _) in casts]
    block_bytes += c_bytes

    n_lhs, n_dots, n_extras, n_casts = len(lhs), len(dots), len(extras), len(casts)
    cw = bn // n_chunks

    def body(*refs):
        a_refs = refs[:n_lhs]
        w_refs = refs[n_lhs:n_lhs + n_dots]
        e_refs = refs[n_lhs + n_dots:n_lhs + n_dots + n_extras]
        ci_refs = refs[n_lhs + n_dots + n_extras:n_lhs + n_dots + n_extras + n_casts]
        o_ref = refs[n_lhs + n_dots + n_extras + n_casts]
        co_refs = refs[n_lhs + n_dots + n_extras + n_casts + 1:]
        for ci, co in zip(ci_refs, co_refs):
            co[...] = ci[...].astype(BF16)
        for c in range(n_chunks):
            cols = slice(c * cw, (c + 1) * cw)
            accs = [jnp.dot(a_refs[d[0]][...], w_ref[:, cols], preferred_element_type=F32)
                    for d, w_ref in zip(dots, w_refs)]
            epilogue(accs, e_refs, o_ref, cols)

    outs = pl.pallas_call(
        body,
        out_shape=[jax.ShapeDtypeStruct((m, n_out), out_dtype)] + c_shapes,
        grid=grid,
        in_specs=in_specs,
        out_specs=[pl.BlockSpec((bm, bn), lambda g0, g1: ij(g0, g1))] + c_out,
        compiler_params=_params(("parallel", "parallel"), block_bytes,
                                temp_bytes=(n_dots + 1) * bm * cw * 4),
        name=name,
    )(*args)
    return outs


def _gelu_tanh(x):
    c = math.sqrt(2.0 / math.pi)
    return 0.5 * x * (1.0 + jnp.tanh(c * (x + 0.044715 * (x * x * x))))


def _sigmoid(x):
    return 0.5 * jnp.tanh(0.5 * x) + 0.5


def _matmul_residual_ln(name, a, w, resid, alpha, gamma, beta, bm, bk):
    m, kdim = a.shape
    n = w.shape[1]
    nk = kdim // bk
    nks = nk - 1
    ln_rows = bm // LN_STEPS
    n_tiles = m // bm
    assert m % bm == 0 and kdim % bk == 0 and ln_rows % LN_ROWS == 0
    grid = (n_tiles + 1, nks + LN_STEPS)

    is_acc = lambda t, s: jnp.logical_and(s < nks, t >= 1)
    a_idx = lambda t, s: (jnp.where(is_acc(t, s), t - 1, jnp.minimum(t, n_tiles - 1)),
                          jnp.where(is_acc(t, s), s + 1, 0))
    w_idx = lambda t, s: (jnp.where(is_acc(t, s), s + 1, 0), 0)
    ln_idx = lambda t, s: (jnp.maximum(t - 1, 0) * LN_STEPS
                           + jnp.where(t >= 1, jnp.maximum(s - nks, 0), 0), 0)

    def body(a_ref, w_ref, r_ref, g_ref, b_ref, o32_ref, o16_ref, acc_ref):
        t = pl.program_id(0)
        s = pl.program_id(1)
        in_ln = s >= nks

        @pl.when(is_acc(t, s))
        def _():
            acc_ref[...] += jnp.dot(a_ref[...], w_ref[...], preferred_element_type=F32)

        def ln_step(normalise, refill):
            base = pl.multiple_of((s - nks) * ln_rows, ln_rows)
            if normalise:
                g = g_ref[...]
                b = b_ref[...]
                for grp in range(ln_rows // LN_ROWS):
                    src = pl.ds(base + grp * LN_ROWS, LN_ROWS)
                    dst = pl.ds(grp * LN_ROWS, LN_ROWS)
                    y = alpha * r_ref[dst, :] + acc_ref[src, :]
                    mu = jnp.mean(y, axis=-1, keepdims=True)
                    d = y - mu
                    var = jnp.mean(d * d, axis=-1, keepdims=True)
                    out = d * lax.rsqrt(var + EPS) * g + b
                    o32_ref[dst, :] = out
                    o16_ref[dst, :] = out.astype(BF16)
            if refill:
                rows = pl.ds(base, ln_rows)
                acc_ref[rows, :] = jnp.dot(a_ref[rows, :], w_ref[...], preferred_element_type=F32)

        @pl.when(jnp.logical_and(in_ln, t == 0))
        def _():
            ln_step(False, True)

        @pl.when(jnp.logical_and(in_ln, jnp.logical_and(t >= 1, t < n_tiles)))
        def _():
            ln_step(True, True)

        @pl.when(jnp.logical_and(in_ln, t == n_tiles))
        def _():
            ln_step(True, False)

    vec = pl.BlockSpec((1, n), lambda t, s: (0, 0))
    ln_rows_spec = pl.BlockSpec((ln_rows, n), ln_idx)
    block_bytes = _nbytes((bm, bk), BF16) + _nbytes((bk, n), BF16) \
        + 2 * _nbytes((ln_rows, n), F32) + _nbytes((ln_rows, n), BF16)
    return pl.pallas_call(
        body,
        out_shape=[jax.ShapeDtypeStruct((m, n), F32), jax.ShapeDtypeStruct((m, n), BF16)],
        grid=grid,
        in_specs=[pl.BlockSpec((bm, bk), a_idx), pl.BlockSpec((bk, n), w_idx),
                  ln_rows_spec, vec, vec],
        out_specs=[ln_rows_spec, ln_rows_spec],
        scratch_shapes=[pltpu.VMEM((bm, n), F32)],
        compiler_params=_params(("arbitrary", "arbitrary"), block_bytes, scratch_bytes=bm * n * 4,
                                temp_bytes=2 * 1024 * 1024),
        name=name,
    )(a, w, resid, gamma, beta)


def _retention_body(q_ref, k_ref, v_ref, g_ref, d_ref, qd_ref, kd_ref, gain_ref, o_ref):
    seq = q_ref.shape[0]
    blk = d_ref.shape[0]
    dmat = d_ref[...]
    qdec = qd_ref[...]
    kdec = kd_ref[...]
    block_decay = qdec[blk - 1:blk, :]
    gain = gain_ref[...]
    state = jnp.zeros((DK_A, DK_A), F32)
    for n in range(seq // blk):
        rows = pl.ds(n * blk, blk)
        q = q_ref[rows, :]
        k = k_ref[rows, :]
        v = v_ref[rows, :]
        s = lax.dot_general(q, k, (((1,), (1,)), ((), ())), preferred_element_type=F32)
        o = jnp.dot((s * dmat).astype(BF16), v, preferred_element_type=F32)
        q_in = (q.astype(F32) * qdec).astype(BF16)
        o = o + jnp.dot(q_in, state.astype(BF16), preferred_element_type=F32)
        k_out_t = (k.astype(F32) * kdec).T.astype(BF16)
        state = state * block_decay + jnp.dot(k_out_t, v, preferred_element_type=F32)
        mu = jnp.mean(o, axis=-1, keepdims=True)
        d = o - mu
        var = jnp.mean(d * d, axis=-1, keepdims=True)
        y = d * lax.rsqrt(var + EPS) * gain
        o_ref[rows, :] = (g_ref[rows, :].astype(F32) * y).astype(o_ref.dtype)


def _retention(q, k, v, g, dmat, qdec, kdec, gain, batch, seq):
    m = q.shape[0]
    blk = dmat.shape[1]
    seq_head = pl.BlockSpec((seq, DK_A), lambda b, h: (b, h))
    head = lambda b, h: (h, 0, 0)
    block_bytes = 5 * _nbytes((seq, DK_A), BF16) + _nbytes((blk, blk), F32) + 2 * _nbytes((blk, DK_A), F32)
    return pl.pallas_call(
        _retention_body,
        out_shape=jax.ShapeDtypeStruct((m, W_A), BF16),
        grid=(batch, H_A),
        in_specs=[seq_head, seq_head, seq_head, seq_head,
                  pl.BlockSpec((None, blk, blk), head),
                  pl.BlockSpec((None, blk, DK_A), head),
                  pl.BlockSpec((None, blk, DK_A), head),
                  pl.BlockSpec((1, DK_A), lambda b, h: (0, h))],
        out_specs=seq_head,
        compiler_params=_params(("parallel", "parallel"), block_bytes, temp_bytes=8 * blk * blk * 4),
        name="retention",
    )(q, k, v, g, dmat, qdec, kdec, gain)


def _rglru_body(x_ref, gg_ref, cw_ref, cb_ref, wa_ref, ba_ref, wx_ref, bx_ref, lam_ref, o_ref,
                xpad, a_s, u_s, h_s):
    nb, tt, _ = x_ref.shape
    halo = (CONV_K - 1) * nb
    t = pl.program_id(1)

    @pl.when(t == 0)
    def _():
        xpad[0:halo, :] = jnp.zeros((halo, BW_B), F32)
        h_s[...] = jnp.zeros_like(h_s)

    xpad[halo:, :] = jnp.swapaxes(x_ref[...], 0, 1).reshape(tt * nb, BW_B)
    cw = cw_ref[...]
    xc2 = cb_ref[...]
    for tap in range(CONV_K):
        xc2 = xc2 + xpad[tap * nb:(tap + tt) * nb, :] * cw[tap:tap + 1, :]
    xpad[0:halo, :] = xpad[tt * nb:tt * nb + halo, :]

    xb = xc2.astype(BF16)
    r = _sigmoid(jnp.dot(xb, wa_ref[...].astype(BF16), preferred_element_type=F32) + ba_ref[...])
    ig = _sigmoid(jnp.dot(xb, wx_ref[...].astype(BF16), preferred_element_type=F32) + bx_ref[...])
    neg_lam = -lam_ref[...]
    softplus = jnp.maximum(neg_lam, 0.0) + jnp.log(1.0 + jnp.exp(-jnp.abs(neg_lam)))
    a = jnp.exp((-C_RG * softplus) * r)
    a_s[...] = a
    u_s[...] = jnp.sqrt(1.0 - a * a) * (ig * xc2)

    def step(s, h):
        rows = pl.ds(pl.multiple_of(s * nb, nb), nb)
        h = a_s[rows, :] * h + u_s[rows, :]
        u_s[rows, :] = h
        return h

    h_s[...] = lax.fori_loop(0, tt, step, h_s[...], unroll=8)
    hs = jnp.swapaxes(u_s[...].reshape(tt, nb, BW_B), 0, 1)
    o_ref[...] = (hs * gg_ref[...]).astype(o_ref.dtype)


def _rglru(xr, gg, conv_w, conv_b, wa, ba, wx, bx, lam, batch, seq):
    tt = _tile(seq, SCAN_TILE)
    xr3 = xr.reshape(batch, seq, W_B)
    gg3 = gg.reshape(batch, seq, W_B)
    tile = pl.BlockSpec((batch, tt, BW_B), lambda h, t: (0, t, h))
    vec = pl.BlockSpec((1, BW_B), lambda h, t: (0, h))
    gate_w = pl.BlockSpec((None, BW_B, BW_B), lambda h, t: (h, 0, 0))
    block_bytes = 2 * _nbytes((batch, tt, BW_B), F32) + _nbytes((batch, tt, BW_B), BF16) \
        + 2 * _nbytes((BW_B, BW_B), F32)
    scratch = [pltpu.VMEM(((tt + CONV_K - 1) * batch, BW_B), F32),
               pltpu.VMEM((batch * tt, BW_B), F32),
               pltpu.VMEM((batch * tt, BW_B), F32),
               pltpu.VMEM((batch, BW_B), F32)]
    scratch_bytes = 3 * _nbytes((batch, tt + SUBLANES, BW_B), F32)
    out = pl.pallas_call(
        _rglru_body,
        out_shape=jax.ShapeDtypeStruct((batch, seq, W_B), BF16),
        grid=(H_B, seq // tt),
        in_specs=[tile, tile,
                  pl.BlockSpec((CONV_K, BW_B), lambda h, t: (0, h)),
                  vec, gate_w, vec, gate_w, vec, vec],
        out_specs=tile,
        scratch_shapes=scratch,
        compiler_params=_params(("parallel", "arbitrary"), block_bytes, scratch_bytes,
                                temp_bytes=8 * batch * tt * BW_B * 4),
        name="rglru",
    )(xr3, gg3, conv_w, conv_b, wa, ba, wx, bx, lam)
    return out.reshape(batch * seq, W_B)


def _retention_tables(blk):
    lg = jnp.log1p(-jnp.exp2(-5.0 - jnp.arange(H_A, dtype=F32)))
    idx = jnp.arange(blk, dtype=F32)
    diff = jnp.abs(idx[:, None] - idx[None, :])
    chunk = jnp.arange(blk) // CHUNK
    visible = chunk[None, :] <= chunk[:, None]
    dmat = jnp.where(visible[None], jnp.exp(lg[:, None, None] * diff[None]), 0.0)
    qdec = jnp.exp(lg[:, None] * (idx + 1.0))
    kdec = jnp.exp(lg[:, None] * (blk - 1.0 - idx))
    bcast = lambda t: jnp.broadcast_to(t[:, :, None], (H_A, blk, DK_A))
    return dmat.astype(F32), bcast(qdec), bcast(kdec)


def _rotary_tables(seq):
    pos = jnp.arange(seq, dtype=F32)
    inv_freq = ROPE_BASE ** (-jnp.arange(0, DK_A, 2, dtype=F32) / DK_A)
    ang = pos[:, None] * inv_freq[None, :]
    cos, sin = jnp.cos(ang), jnp.sin(ang)
    return jnp.concatenate([cos, cos], axis=-1), jnp.concatenate([-sin, sin], axis=-1)


def kernel(x, w_in, ret_gn_w, conv_w, conv_b, rg_wa, rg_ba, rg_wx, rg_bx, rg_lambda, w_br_a, w_br_b,
           b_gate, w_o, ln1_g, ln1_b, w_up, w_down, ln2_g, ln2_b):
    batch, seq, d_model = x.shape
    depth = w_in.shape[0]
    d_ff = w_up.shape[-1]
    m = batch * seq
    alpha = (2.0 * depth) ** 0.25
    assert w_in.shape[-1] == 4 * W_A + 2 * W_B + 2 * d_model
    assert batch == SUBLANES, "the scan packs the batch onto the sublanes of one vreg"

    blk = _tile(seq, RET_BLOCK)
    dmat, qdec, kdec = _retention_tables(blk)
    cos2, sin2 = _rotary_tables(seq)

    bm = _tile(seq, 1024)
    bn = 1024
    assert W_A == bn and W_B == bn
    pos_blocks = seq // bm
    pos_map = lambda i, j: (i % pos_blocks, 0)
    k_scale = DK_A ** -0.5

    def rotary_epilogue(scale):
        def epilogue(accs, e_refs, o_ref, cols):
            cos_ref, sin_ref = e_refs
            c, s = cos_ref[...], sin_ref[...]
            acc = accs[0]
            for hh in range(acc.shape[1] // DK_A):
                t = acc[:, hh * DK_A:(hh + 1) * DK_A]
                rot = t * c + pltpu.roll(t, DK_A // 2, 1) * s
                if scale is not None:
                    rot = rot * scale
                lo = cols.start + hh * DK_A
                o_ref[:, lo:lo + DK_A] = rot.astype(o_ref.dtype)
        return epilogue

    def cast_epilogue(accs, e_refs, o_ref, cols):
        o_ref[:, cols] = accs[0].astype(o_ref.dtype)

    def silu_epilogue(accs, e_refs, o_ref, cols):
        o_ref[:, cols] = (accs[0] * _sigmoid(accs[0])).astype(o_ref.dtype)

    def gelu_epilogue(accs, e_refs, o_ref, cols):
        o_ref[:, cols] = _gelu_tanh(accs[0])

    def merge_epilogue(accs, e_refs, o_ref, cols):
        ga, gb, ya, yb = accs
        ba_ref, bb_ref = e_refs
        out = _sigmoid(ga + ba_ref[:, cols]) * ya + _sigmoid(gb + bb_ref[:, cols]) * yb
        o_ref[:, cols] = out.astype(o_ref.dtype)

    def relu2_epilogue(accs, e_refs, o_ref, cols):
        h = jnp.maximum(accs[0], 0.0)
        o_ref[:, cols] = (h * h).astype(o_ref.dtype)

    x32 = x.reshape(m, d_model)
    x16 = x32.astype(BF16)
    w_in_b = w_in[0].astype(BF16)
    bn_merge = 512
    bm_merge = _tile(m, 512)
    bm_ln = _tile(m, 1024)
    gate_a_off = (4 * W_A + 2 * W_B) // bn_merge
    gate_b_off = gate_a_off + d_model // bn_merge
    rot_extras = [(cos2, (bm, DK_A), pos_map), (sin2, (bm, DK_A), pos_map)]

    def in_proj(name, col_block, extras, epilogue, dtype, casts=()):
        return _fused_matmul(name, [x16], [(0, w_in_b, col_block)], extras, epilogue, bn, dtype,
                             bm, bn, rhs_resident=True, n_chunks=2, casts=casts)

    for l in range(depth):
        q, w_br_a_b = in_proj("in_proj_q", 0, rot_extras, rotary_epilogue(None), BF16, [(w_br_a, l)])
        k, w_br_b_b = in_proj("in_proj_k", 1, rot_extras, rotary_epilogue(k_scale), BF16, [(w_br_b, l)])
        v, = in_proj("in_proj_v", 2, [], cast_epilogue, BF16)
        g, = in_proj("in_proj_g", 3, [], silu_epilogue, BF16)
        xr, = in_proj("in_proj_xr", 4, [], cast_epilogue, F32)
        gg, = in_proj("in_proj_gr", 5, [], gelu_epilogue, F32)

        ret = _retention(q, k, v, g, dmat, qdec, kdec, ret_gn_w[l].reshape(1, W_A), batch, seq)
        rec = _rglru(xr, gg, conv_w[l], conv_b[l].reshape(1, W_B), rg_wa[l], rg_ba[l].reshape(1, W_B),
                     rg_wx[l], rg_bx[l].reshape(1, W_B), rg_lambda[l].reshape(1, W_B), batch, seq)

        bg = b_gate[l].reshape(1, 2 * d_model)
        merged, w_up_b, w_o_b = _fused_matmul(
            "merge", [x16, ret, rec],
            [(0, w_in_b, gate_a_off), (0, w_in_b, gate_b_off), (1, w_br_a_b, 0), (2, w_br_b_b, 0)],
            [(bg, (1, bn_merge), lambda i, j: (0, j)),
             (bg, (1, bn_merge), lambda i, j: (0, d_model // bn_merge + j))],
            merge_epilogue, d_model, BF16, bm_merge, bn_merge, rhs_resident=True, n_chunks=2,
            casts=[(w_up, l), (w_o, l)])

        x32, x16 = _matmul_residual_ln(
            "out_proj_ln", merged, w_o_b, x32, alpha, ln1_g[l].reshape(1, d_model),
            ln1_b[l].reshape(1, d_model), bm_ln, _tile(d_model, 1024))

        next_w_in = [(w_in, l + 1)] if l + 1 < depth else []
        hid, w_down_b, *w_in_next = _fused_matmul(
            "mlp_up", [x16], [(0, w_up_b, 0)], [], relu2_epilogue, d_ff, BF16, bm, bn,
            rhs_resident=False, n_chunks=2, casts=[(w_down, l)] + next_w_in)

        x32, x16 = _matmul_residual_ln(
            "mlp_down_ln", hid, w_down_b, x32, alpha, ln2_g[l].reshape(1, d_model),
            ln2_b[l].reshape(1, d_model), bm_ln, _tile(d_ff, 1024))
        if w_in_next:
            w_in_b = w_in_next[0]

    return x32.reshape(batch, seq, d_model)
```

```python
import math

import jax
import jax.numpy as jnp
from jax import lax
from jax.experimental import pallas as pl
from jax.experimental.pallas import tpu as pltpu

F32 = jnp.float32
BF16 = jnp.bfloat16

H_A = 8
DK_A = 128
W_A = H_A * DK_A
H_B = 8
BW_B = 128
W_B = H_B * BW_B
CONV_K = 4
C_RG = 8.0
CHUNK = 64
EPS = 1e-5
ROPE_BASE = 10000.0

V7X_VMEM_REQUEST_CAP = 60 * 1024 * 1024
LANES = 128
SUBLANES = 8
BF16_SUBLANES = 16

RET_BLOCK = 256
SCAN_TILE = 256
LN_ROWS = 16


def _nbytes(shape, dtype):
    return math.prod(shape) * jnp.dtype(dtype).itemsize


def _params(semantics, block_bytes, scratch_bytes=0, temp_bytes=0):
    need = 2 * block_bytes + scratch_bytes + temp_bytes + 4 * 1024 * 1024
    limit = min(max(need, 32 * 1024 * 1024), V7X_VMEM_REQUEST_CAP)
    return pltpu.CompilerParams(dimension_semantics=semantics, vmem_limit_bytes=limit)


def _tile(dim, pref):
    t = min(dim, pref)
    while dim % t:
        t //= 2
    return t


def _cast_plan(rows, cols, nsteps):
    ncb = 1
    while ncb <= nsteps:
        if cols % (ncb * LANES) == 0 and nsteps % ncb == 0:
            nrb = nsteps // ncb
            if rows % nrb == 0 and (rows // nrb) % BF16_SUBLANES == 0:
                return rows // nrb, cols // ncb
        ncb *= 2
    raise ValueError(f"no cast blocking for {(rows, cols)} in {nsteps} steps")


def _cast_specs(casts, nsteps, step_of):
    in_specs, out_specs, out_shapes, nbytes = [], [], [], 0
    for (w, lead) in casts:
        _, rows, cols = w.shape
        rb, cb = _cast_plan(rows, cols, nsteps)
        ncb = cols // cb

        def blk(*g, ncb=ncb):
            lin = step_of(*g)
            return lin // ncb, lin % ncb

        in_specs.append(pl.BlockSpec((None, rb, cb), lambda *g, lead=lead, blk=blk: (lead,) + blk(*g)))
        out_specs.append(pl.BlockSpec((rb, cb), lambda *g, blk=blk: blk(*g)))
        out_shapes.append(jax.ShapeDtypeStruct((rows, cols), BF16))
        nbytes += _nbytes((rb, cb), F32) + _nbytes((rb, cb), BF16)
    return in_specs, out_specs, out_shapes, nbytes


def _fused_matmul(name, lhs, dots, extras, epilogue, n_out, out_dtype, bm, bn, rhs_resident,
                  n_chunks=1, casts=()):
    m = lhs[0].shape[0]
    assert m % bm == 0 and n_out % bn == 0 and bn % n_chunks == 0
    if rhs_resident:
        grid = (n_out // bn, m // bm)
        ij = lambda g0, g1: (g1, g0)
    else:
        grid = (m // bm, n_out // bn)
        ij = lambda g0, g1: (g0, g1)

    in_specs, args, block_bytes = [], [], 0
    for a in lhs:
        k = a.shape[1]
        in_specs.append(pl.BlockSpec((bm, k), lambda g0, g1: (ij(g0, g1)[0], 0)))
        args.append(a)
        block_bytes += _nbytes((bm, k), a.dtype)
    for (_, w, off) in dots:
        k = w.shape[0]
        in_specs.append(pl.BlockSpec((k, bn), lambda g0, g1, off=off: (0, off + ij(g0, g1)[1])))
        args.append(w)
        block_bytes += _nbytes((k, bn), w.dtype)
    for (e, bshape, imap) in extras:
        in_specs.append(pl.BlockSpec(bshape, lambda g0, g1, imap=imap: imap(*ij(g0, g1))))
        args.append(e)
        block_bytes += _nbytes(bshape, e.dtype)
    block_bytes += _nbytes((bm, bn), out_dtype)
    c_in, c_out, c_shapes, c_bytes = _cast_specs(casts, grid[0] * grid[1],
                                                  lambda g0, g1: g0 * grid[1] + g1)
    in_specs += c_in
    args += [w for (w, _) in casts]
    block_bytes += c_bytes

    n_lhs, n_dots, n_extras, n_casts = len(lhs), len(dots), len(extras), len(casts)
    cw = bn // n_chunks

    def body(*refs):
        a_refs = refs[:n_lhs]
        w_refs = refs[n_lhs:n_lhs + n_dots]
        e_refs = refs[n_lhs + n_dots:n_lhs + n_dots + n_extras]
        ci_refs = refs[n_lhs + n_dots + n_extras:n_lhs + n_dots + n_extras + n_casts]
        o_ref = refs[n_lhs + n_dots + n_extras + n_casts]
        co_refs = refs[n_lhs + n_dots + n_extras + n_casts + 1:]
        for ci, co in zip(ci_refs, co_refs):
            co[...] = ci[...].astype(BF16)
        for c in range(n_chunks):
            cols = slice(c * cw, (c + 1) * cw)
            accs = [jnp.dot(a_refs[d[0]][...], w_ref[:, cols], preferred_element_type=F32)
                    for d, w_ref in zip(dots, w_refs)]
            epilogue(accs, e_refs, o_ref, cols)

    outs = pl.pallas_call(
        body,
        out_shape=[jax.ShapeDtypeStruct((m, n_out), out_dtype)] + c_shapes,
        grid=grid,
        in_specs=in_specs,
        out_specs=[pl.BlockSpec((bm, bn), lambda g0, g1: ij(g0, g1))] + c_out,
        compiler_params=_params(("parallel", "parallel"), block_bytes,
                                temp_bytes=(n_dots + 1) * bm * cw * 4),
        name=name,
    )(*args)
    return outs


def _gelu_tanh(x):
    c = math.sqrt(2.0 / math.pi)
    return 0.5 * x * (1.0 + jnp.tanh(c * (x + 0.044715 * (x * x * x))))


def _sigmoid(x):
    return 0.5 * jnp.tanh(0.5 * x) + 0.5


def _matmul_residual_ln(name, a, w, resid, alpha, gamma, beta, bm, bk):
    m, kdim = a.shape
    n = w.shape[1]
    nk = kdim // bk
    rc = bm // nk
    n_tiles = m // bm
    n_pairs = n_tiles // 2
    assert m % bm == 0 and kdim % bk == 0 and n_tiles % 2 == 0 and bm % nk == 0 and rc % LN_ROWS == 0
    grid = (n_pairs + 1, 2 * nk)

    half_of = lambda s: jnp.where(s >= nk, 1, 0)
    kblock_of = lambda s: s - nk * half_of(s)

    def a_idx(p, s):
        tile = jnp.minimum(2 * p + half_of(s), n_tiles - 1)
        return tile, jnp.where(p < n_pairs, kblock_of(s), nk - 1)

    def w_idx(p, s):
        return jnp.where(p < n_pairs, kblock_of(s), nk - 1), 0

    def ln_idx(p, s):
        tile = 2 * p + half_of(s) - 1
        idx = tile * nk + kblock_of(s)
        return jnp.clip(jnp.where(tile < 0, 0, idx), 0, n_tiles * nk - 1), 0

    def body(a_ref, w_ref, r_ref, g_ref, b_ref, o32_ref, o16_ref, acc0, acc1):
        p = pl.program_id(0)
        s = pl.program_id(1)
        active = p < n_pairs

        def product():
            return jnp.dot(a_ref[...], w_ref[...], preferred_element_type=F32)

        def normalise(acc_src, kb):
            g = g_ref[...]
            b = b_ref[...]
            base = pl.multiple_of(kb * rc, rc)
            for grp in range(rc // LN_ROWS):
                src = pl.ds(base + grp * LN_ROWS, LN_ROWS)
                dst = pl.ds(grp * LN_ROWS, LN_ROWS)
                y = alpha * r_ref[dst, :] + acc_src[src, :]
                mu = jnp.mean(y, axis=-1, keepdims=True)
                d = y - mu
                var = jnp.mean(d * d, axis=-1, keepdims=True)
                out = d * lax.rsqrt(var + EPS) * g + b
                o32_ref[dst, :] = out
                o16_ref[dst, :] = out.astype(BF16)

        first_pair = p == 0
        later_pair = jnp.logical_and(active, jnp.logical_not(first_pair))
        mid_first_half = jnp.logical_and(s > 0, s < nk)

        @pl.when(jnp.logical_and(s == 0, first_pair))
        def _():
            acc0[...] = product()

        @pl.when(jnp.logical_and(mid_first_half, first_pair))
        def _():
            acc0[...] += product()

        @pl.when(jnp.logical_and(s == 0, later_pair))
        def _():
            acc0[...] = product()
            normalise(acc1, 0)

        @pl.when(jnp.logical_and(mid_first_half, later_pair))
        def _():
            acc0[...] += product()
            normalise(acc1, s)

        @pl.when(jnp.logical_and(s == nk, active))
        def _():
            acc1[...] = product()
            normalise(acc0, 0)

        @pl.when(jnp.logical_and(s > nk, active))
        def _():
            acc1[...] += product()
            normalise(acc0, s - nk)

        @pl.when(jnp.logical_and(s < nk, jnp.logical_not(active)))
        def _():
            normalise(acc1, s)

    vec = pl.BlockSpec((1, n), lambda p, s: (0, 0))
    ln_rows_spec = pl.BlockSpec((rc, n), ln_idx)
    block_bytes = _nbytes((bm, bk), BF16) + _nbytes((bk, n), BF16) \
        + 2 * _nbytes((rc, n), F32) + _nbytes((rc, n), BF16)
    return pl.pallas_call(
        body,
        out_shape=[jax.ShapeDtypeStruct((m, n), F32), jax.ShapeDtypeStruct((m, n), BF16)],
        grid=grid,
        in_specs=[pl.BlockSpec((bm, bk), a_idx), pl.BlockSpec((bk, n), w_idx),
                  ln_rows_spec, vec, vec],
        out_specs=[ln_rows_spec, ln_rows_spec],
        scratch_shapes=[pltpu.VMEM((bm, n), F32), pltpu.VMEM((bm, n), F32)],
        compiler_params=_params(("arbitrary", "arbitrary"), block_bytes,
                                scratch_bytes=2 * bm * n * 4, temp_bytes=2 * 1024 * 1024),
        name=name,
    )(a, w, resid, gamma, beta)


def _retention_body(q_ref, k_ref, v_ref, g_ref, d_ref, qd_ref, kd_ref, gain_ref, o_ref):
    seq = q_ref.shape[0]
    blk = d_ref.shape[0]
    dmat = d_ref[...]
    qdec = qd_ref[...]
    kdec = kd_ref[...]
    block_decay = qdec[blk - 1:blk, :]
    gain = gain_ref[...]
    state = jnp.zeros((DK_A, DK_A), F32)
    for n in range(seq // blk):
        rows = pl.ds(n * blk, blk)
        q = q_ref[rows, :]
        k = k_ref[rows, :]
        v = v_ref[rows, :]
        s = lax.dot_general(q, k, (((1,), (1,)), ((), ())), preferred_element_type=F32)
        o = jnp.dot((s * dmat).astype(BF16), v, preferred_element_type=F32)
        q_in = (q.astype(F32) * qdec).astype(BF16)
        o = o + jnp.dot(q_in, state.astype(BF16), preferred_element_type=F32)
        k_out_t = (k.astype(F32) * kdec).T.astype(BF16)
        state = state * block_decay + jnp.dot(k_out_t, v, preferred_element_type=F32)
        mu = jnp.mean(o, axis=-1, keepdims=True)
        d = o - mu
        var = jnp.mean(d * d, axis=-1, keepdims=True)
        y = d * lax.rsqrt(var + EPS) * gain
        o_ref[rows, :] = (g_ref[rows, :].astype(F32) * y).astype(o_ref.dtype)


def _retention(q, k, v, g, dmat, qdec, kdec, gain, batch, seq):
    m = q.shape[0]
    blk = dmat.shape[1]
    seq_head = pl.BlockSpec((seq, DK_A), lambda b, h: (b, h))
    head = lambda b, h: (h, 0, 0)
    block_bytes = 5 * _nbytes((seq, DK_A), BF16) + _nbytes((blk, blk), F32) + 2 * _nbytes((blk, DK_A), F32)
    return pl.pallas_call(
        _retention_body,
        out_shape=jax.ShapeDtypeStruct((m, W_A), BF16),
        grid=(batch, H_A),
        in_specs=[seq_head, seq_head, seq_head, seq_head,
                  pl.BlockSpec((None, blk, blk), head),
                  pl.BlockSpec((None, blk, DK_A), head),
                  pl.BlockSpec((None, blk, DK_A), head),
                  pl.BlockSpec((1, DK_A), lambda b, h: (0, h))],
        out_specs=seq_head,
        compiler_params=_params(("parallel", "parallel"), block_bytes, temp_bytes=8 * blk * blk * 4),
        name="retention",
    )(q, k, v, g, dmat, qdec, kdec, gain)


def _rglru_body(x_ref, gg_ref, cw_ref, cb_ref, wa_ref, ba_ref, wx_ref, bx_ref, lam_ref, o_ref,
                xpad, a_s, u_s, h_s):
    nb, tt, _ = x_ref.shape
    halo = (CONV_K - 1) * nb
    t = pl.program_id(1)

    @pl.when(t == 0)
    def _():
        xpad[0:halo, :] = jnp.zeros((halo, BW_B), F32)
        h_s[...] = jnp.zeros_like(h_s)

    xpad[halo:, :] = jnp.swapaxes(x_ref[...], 0, 1).reshape(tt * nb, BW_B)
    cw = cw_ref[...]
    xc2 = cb_ref[...]
    for tap in range(CONV_K):
        xc2 = xc2 + xpad[tap * nb:(tap + tt) * nb, :] * cw[tap:tap + 1, :]
    xpad[0:halo, :] = xpad[tt * nb:tt * nb + halo, :]

    xb = xc2.astype(BF16)
    r = _sigmoid(jnp.dot(xb, wa_ref[...].astype(BF16), preferred_element_type=F32) + ba_ref[...])
    ig = _sigmoid(jnp.dot(xb, wx_ref[...].astype(BF16), preferred_element_type=F32) + bx_ref[...])
    neg_lam = -lam_ref[...]
    softplus = jnp.maximum(neg_lam, 0.0) + jnp.log(1.0 + jnp.exp(-jnp.abs(neg_lam)))
    a = jnp.exp((-C_RG * softplus) * r)
    a_s[...] = a
    u_s[...] = jnp.sqrt(1.0 - a * a) * (ig * xc2)

    def step(s, h):
        rows = pl.ds(pl.multiple_of(s * nb, nb), nb)
        h = a_s[rows, :] * h + u_s[rows, :]
        u_s[rows, :] = h
        return h

    h_s[...] = lax.fori_loop(0, tt, step, h_s[...], unroll=8)
    hs = jnp.swapaxes(u_s[...].reshape(tt, nb, BW_B), 0, 1)
    o_ref[...] = (hs * gg_ref[...]).astype(o_ref.dtype)


def _rglru(xr, gg, conv_w, conv_b, wa, ba, wx, bx, lam, batch, seq):
    tt = _tile(seq, SCAN_TILE)
    xr3 = xr.reshape(batch, seq, W_B)
    gg3 = gg.reshape(batch, seq, W_B)
    tile = pl.BlockSpec((batch, tt, BW_B), lambda h, t: (0, t, h))
    vec = pl.BlockSpec((1, BW_B), lambda h, t: (0, h))
    gate_w = pl.BlockSpec((None, BW_B, BW_B), lambda h, t: (h, 0, 0))
    block_bytes = 2 * _nbytes((batch, tt, BW_B), F32) + _nbytes((batch, tt, BW_B), BF16) \
        + 2 * _nbytes((BW_B, BW_B), F32)
    scratch = [pltpu.VMEM(((tt + CONV_K - 1) * batch, BW_B), F32),
               pltpu.VMEM((batch * tt, BW_B), F32),
               pltpu.VMEM((batch * tt, BW_B), F32),
               pltpu.VMEM((batch, BW_B), F32)]
    scratch_bytes = 3 * _nbytes((batch, tt + SUBLANES, BW_B), F32)
    out = pl.pallas_call(
        _rglru_body,
        out_shape=jax.ShapeDtypeStruct((batch, seq, W_B), BF16),
        grid=(H_B, seq // tt),
        in_specs=[tile, tile,
                  pl.BlockSpec((CONV_K, BW_B), lambda h, t: (0, h)),
                  vec, gate_w, vec, gate_w, vec, vec],
        out_specs=tile,
        scratch_shapes=scratch,
        compiler_params=_params(("parallel", "arbitrary"), block_bytes, scratch_bytes,
                                temp_bytes=8 * batch * tt * BW_B * 4),
        name="rglru",
    )(xr3, gg3, conv_w, conv_b, wa, ba, wx, bx, lam)
    return out.reshape(batch * seq, W_B)


def _retention_tables(blk):
    lg = jnp.log1p(-jnp.exp2(-5.0 - jnp.arange(H_A, dtype=F32)))
    idx = jnp.arange(blk, dtype=F32)
    diff = jnp.abs(idx[:, None] - idx[None, :])
    chunk = jnp.arange(blk) // CHUNK
    visible = chunk[None, :] <= chunk[:, None]
    dmat = jnp.where(visible[None], jnp.exp(lg[:, None, None] * diff[None]), 0.0)
    qdec = jnp.exp(lg[:, None] * (idx + 1.0))
    kdec = jnp.exp(lg[:, None] * (blk - 1.0 - idx))
    bcast = lambda t: jnp.broadcast_to(t[:, :, None], (H_A, blk, DK_A))
    return dmat.astype(F32), bcast(qdec), bcast(kdec)


def _rotary_tables(seq):
    pos = jnp.arange(seq, dtype=F32)
    inv_freq = ROPE_BASE ** (-jnp.arange(0, DK_A, 2, dtype=F32) / DK_A)
    ang = pos[:, None] * inv_freq[None, :]
    cos, sin = jnp.cos(ang), jnp.sin(ang)
    return jnp.concatenate([cos, cos], axis=-1), jnp.concatenate([-sin, sin], axis=-1)


def kernel(x, w_in, ret_gn_w, conv_w, conv_b, rg_wa, rg_ba, rg_wx, rg_bx, rg_lambda, w_br_a, w_br_b,
           b_gate, w_o, ln1_g, ln1_b, w_up, w_down, ln2_g, ln2_b):
    batch, seq, d_model = x.shape
    depth = w_in.shape[0]
    d_ff = w_up.shape[-1]
    m = batch * seq
    alpha = (2.0 * depth) ** 0.25
    assert w_in.shape[-1] == 4 * W_A + 2 * W_B + 2 * d_model
    assert batch == SUBLANES, "the scan packs the batch onto the sublanes of one vreg"

    blk = _tile(seq, RET_BLOCK)
    dmat, qdec, kdec = _retention_tables(blk)
    cos2, sin2 = _rotary_tables(seq)

    bm = _tile(seq, 1024)
    bn = 1024
    assert W_A == bn and W_B == bn
    pos_blocks = seq // bm
    pos_map = lambda i, j: (i % pos_blocks, 0)
    k_scale = DK_A ** -0.5

    def rotary_epilogue(scale):
        def epilogue(accs, e_refs, o_ref, cols):
            cos_ref, sin_ref = e_refs
            c, s = cos_ref[...], sin_ref[...]
            acc = accs[0]
            for hh in range(acc.shape[1] // DK_A):
                t = acc[:, hh * DK_A:(hh + 1) * DK_A]
                rot = t * c + pltpu.roll(t, DK_A // 2, 1) * s
                if scale is not None:
                    rot = rot * scale
                lo = cols.start + hh * DK_A
                o_ref[:, lo:lo + DK_A] = rot.astype(o_ref.dtype)
        return epilogue

    def cast_epilogue(accs, e_refs, o_ref, cols):
        o_ref[:, cols] = accs[0].astype(o_ref.dtype)

    def silu_epilogue(accs, e_refs, o_ref, cols):
        o_ref[:, cols] = (accs[0] * _sigmoid(accs[0])).astype(o_ref.dtype)

    def gelu_epilogue(accs, e_refs, o_ref, cols):
        o_ref[:, cols] = _gelu_tanh(accs[0])

    def merge_epilogue(accs, e_refs, o_ref, cols):
        ga, gb, ya, yb = accs
        ba_ref, bb_ref = e_refs
        out = _sigmoid(ga + ba_ref[:, cols]) * ya + _sigmoid(gb + bb_ref[:, cols]) * yb
        o_ref[:, cols] = out.astype(o_ref.dtype)

    def relu2_epilogue(accs, e_refs, o_ref, cols):
        h = jnp.maximum(accs[0], 0.0)
        o_ref[:, cols] = (h * h).astype(o_ref.dtype)

    x32 = x.reshape(m, d_model)
    x16 = x32.astype(BF16)
    w_in_b = w_in[0].astype(BF16)
    bn_merge = 512
    bm_merge = _tile(m, 512)
    bm_ln = _tile(m, 512)
    gate_a_off = (4 * W_A + 2 * W_B) // bn_merge
    gate_b_off = gate_a_off + d_model // bn_merge
    rot_extras = [(cos2, (bm, DK_A), pos_map), (sin2, (bm, DK_A), pos_map)]

    def in_proj(name, col_block, extras, epilogue, dtype, casts=()):
        return _fused_matmul(name, [x16], [(0, w_in_b, col_block)], extras, epilogue, bn, dtype,
                             bm, bn, rhs_resident=True, n_chunks=2, casts=casts)

    for l in range(depth):
        q, w_br_a_b = in_proj("in_proj_q", 0, rot_extras, rotary_epilogue(None), BF16, [(w_br_a, l)])
        k, w_br_b_b = in_proj("in_proj_k", 1, rot_extras, rotary_epilogue(k_scale), BF16, [(w_br_b, l)])
        v, = in_proj("in_proj_v", 2, [], cast_epilogue, BF16)
        g, = in_proj("in_proj_g", 3, [], silu_epilogue, BF16)
        xr, = in_proj("in_proj_xr", 4, [], cast_epilogue, F32)
        gg, = in_proj("in_proj_gr", 5, [], gelu_epilogue, F32)

        ret = _retention(q, k, v, g, dmat, qdec, kdec, ret_gn_w[l].reshape(1, W_A), batch, seq)
        rec = _rglru(xr, gg, conv_w[l], conv_b[l].reshape(1, W_B), rg_wa[l], rg_ba[l].reshape(1, W_B),
                     rg_wx[l], rg_bx[l].reshape(1, W_B), rg_lambda[l].reshape(1, W_B), batch, seq)

        bg = b_gate[l].reshape(1, 2 * d_model)
        merged, w_up_b, w_o_b = _fused_matmul(
            "merge", [x16, ret, rec],
            [(0, w_in_b, gate_a_off), (0, w_in_b, gate_b_off), (1, w_br_a_b, 0), (2, w_br_b_b, 0)],
            [(bg, (1, bn_merge), lambda i, j: (0, j)),
             (bg, (1, bn_merge), lambda i, j: (0, d_model // bn_merge + j))],
            merge_epilogue, d_model, BF16, bm_merge, bn_merge, rhs_resident=True, n_chunks=2,
            casts=[(w_up, l), (w_o, l)])

        x32, x16 = _matmul_residual_ln(
            "out_proj_ln", merged, w_o_b, x32, alpha, ln1_g[l].reshape(1, d_model),
            ln1_b[l].reshape(1, d_model), bm_ln, _tile(d_model, 1024))

        next_w_in = [(w_in, l + 1)] if l + 1 < depth else []
        hid, w_down_b, *w_in_next = _fused_matmul(
            "mlp_up", [x16], [(0, w_up_b, 0)], [], relu2_epilogue, d_ff, BF16, bm, bn,
            rhs_resident=False, n_chunks=2, casts=[(w_down, l)] + next_w_in)

        x32, x16 = _matmul_residual_ln(
            "mlp_down_ln", hid, w_down_b, x32, alpha, ln2_g[l].reshape(1, d_model),
            ln2_b[l].reshape(1, d_model), bm_ln, _tile(d_ff, 1024))
        if w_in_next:
            w_in_b = w_in_next[0]

    return x32.reshape(batch, seq, d_model)
```

```python
import math

import jax
import jax.numpy as jnp
from jax import lax
from jax.experimental import pallas as pl
from jax.experimental.pallas import tpu as pltpu

F32 = jnp.float32
BF16 = jnp.bfloat16

H_A = 8
DK_A = 128
W_A = H_A * DK_A
H_B = 8
BW_B = 128
W_B = H_B * BW_B
CONV_K = 4
C_RG = 8.0
CHUNK = 64
EPS = 1e-5
ROPE_BASE = 10000.0

V7X_VMEM_REQUEST_CAP = 60 * 1024 * 1024
LANES = 128
SUBLANES = 8
BF16_SUBLANES = 16

RET_BLOCK = 256
LN_ROWS = 16


def _nbytes(shape, dtype):
    return math.prod(shape) * jnp.dtype(dtype).itemsize


def _params(semantics, block_bytes, scratch_bytes=0, temp_bytes=0):
    need = 2 * block_bytes + scratch_bytes + temp_bytes + 4 * 1024 * 1024
    limit = min(max(need, 32 * 1024 * 1024), V7X_VMEM_REQUEST_CAP)
    return pltpu.CompilerParams(dimension_semantics=semantics, vmem_limit_bytes=limit)


def _tile(dim, pref):
    t = min(dim, pref)
    while dim % t:
        t //= 2
    return t


def _cast_plan(rows, cols, nsteps):
    ncb = 1
    while ncb <= nsteps:
        if cols % (ncb * LANES) == 0 and nsteps % ncb == 0:
            nrb = nsteps // ncb
            if rows % nrb == 0 and (rows // nrb) % BF16_SUBLANES == 0:
                return rows // nrb, cols // ncb
        ncb *= 2
    raise ValueError(f"no cast blocking for {(rows, cols)} in {nsteps} steps")


def _cast_specs(casts, nsteps, step_of):
    in_specs, out_specs, out_shapes, nbytes = [], [], [], 0
    for (w, lead) in casts:
        _, rows, cols = w.shape
        rb, cb = _cast_plan(rows, cols, nsteps)
        ncb = cols // cb

        def blk(*g, ncb=ncb):
            lin = step_of(*g)
            return lin // ncb, lin % ncb

        in_specs.append(pl.BlockSpec((None, rb, cb), lambda *g, lead=lead, blk=blk: (lead,) + blk(*g)))
        out_specs.append(pl.BlockSpec((rb, cb), lambda *g, blk=blk: blk(*g)))
        out_shapes.append(jax.ShapeDtypeStruct((rows, cols), BF16))
        nbytes += _nbytes((rb, cb), F32) + _nbytes((rb, cb), BF16)
    return in_specs, out_specs, out_shapes, nbytes


def _fused_matmul(name, lhs, dots, extras, epilogue, n_out, out_dtype, bm, bn, rhs_resident,
                  n_chunks=1, casts=()):
    m = lhs[0].shape[0]
    assert m % bm == 0 and n_out % bn == 0 and bn % n_chunks == 0
    if rhs_resident:
        grid = (n_out // bn, m // bm)
        ij = lambda g0, g1: (g1, g0)
    else:
        grid = (m // bm, n_out // bn)
        ij = lambda g0, g1: (g0, g1)

    in_specs, args, block_bytes = [], [], 0
    for a in lhs:
        k = a.shape[1]
        in_specs.append(pl.BlockSpec((bm, k), lambda g0, g1: (ij(g0, g1)[0], 0)))
        args.append(a)
        block_bytes += _nbytes((bm, k), a.dtype)
    for (_, w, off) in dots:
        k = w.shape[0]
        in_specs.append(pl.BlockSpec((k, bn), lambda g0, g1, off=off: (0, off + ij(g0, g1)[1])))
        args.append(w)
        block_bytes += _nbytes((k, bn), w.dtype)
    for (e, bshape, imap) in extras:
        in_specs.append(pl.BlockSpec(bshape, lambda g0, g1, imap=imap: imap(*ij(g0, g1))))
        args.append(e)
        block_bytes += _nbytes(bshape, e.dtype)
    block_bytes += _nbytes((bm, bn), out_dtype)
    c_in, c_out, c_shapes, c_bytes = _cast_specs(casts, grid[0] * grid[1],
                                                  lambda g0, g1: g0 * grid[1] + g1)
    in_specs += c_in
    args += [w for (w, _) in casts]
    block_bytes += c_bytes

    n_lhs, n_dots, n_extras, n_casts = len(lhs), len(dots), len(extras), len(casts)
    cw = bn // n_chunks

    def body(*refs):
        a_refs = refs[:n_lhs]
        w_refs = refs[n_lhs:n_lhs + n_dots]
        e_refs = refs[n_lhs + n_dots:n_lhs + n_dots + n_extras]
        ci_refs = refs[n_lhs + n_dots + n_extras:n_lhs + n_dots + n_extras + n_casts]
        o_ref = refs[n_lhs + n_dots + n_extras + n_casts]
        co_refs = refs[n_lhs + n_dots + n_extras + n_casts + 1:]
        for ci, co in zip(ci_refs, co_refs):
            co[...] = ci[...].astype(BF16)
        for c in range(n_chunks):
            cols = slice(c * cw, (c + 1) * cw)
            accs = [jnp.dot(a_refs[d[0]][...], w_ref[:, cols], preferred_element_type=F32)
                    for d, w_ref in zip(dots, w_refs)]
            epilogue(accs, e_refs, o_ref, cols)

    outs = pl.pallas_call(
        body,
        out_shape=[jax.ShapeDtypeStruct((m, n_out), out_dtype)] + c_shapes,
        grid=grid,
        in_specs=in_specs,
        out_specs=[pl.BlockSpec((bm, bn), lambda g0, g1: ij(g0, g1))] + c_out,
        compiler_params=_params(("parallel", "parallel"), block_bytes,
                                temp_bytes=(n_dots + 1) * bm * cw * 4),
        name=name,
    )(*args)
    return outs


def _gelu_tanh(x):
    c = math.sqrt(2.0 / math.pi)
    return 0.5 * x * (1.0 + jnp.tanh(c * (x + 0.044715 * (x * x * x))))


def _sigmoid(x):
    return 0.5 * jnp.tanh(0.5 * x) + 0.5


def _matmul_residual_ln(name, a, w, resid, alpha, gamma, beta, bm, bk):
    m, kdim = a.shape
    n = w.shape[1]
    nk = kdim // bk
    rc = bm // nk
    n_tiles = m // bm
    n_pairs = n_tiles // 2
    assert m % bm == 0 and kdim % bk == 0 and n_tiles % 2 == 0 and bm % nk == 0 and rc % LN_ROWS == 0
    grid = (n_pairs + 1, 2 * nk)

    half_of = lambda s: jnp.where(s >= nk, 1, 0)
    kblock_of = lambda s: s - nk * half_of(s)

    def a_idx(p, s):
        tile = jnp.minimum(2 * p + half_of(s), n_tiles - 1)
        return tile, jnp.where(p < n_pairs, kblock_of(s), nk - 1)

    def w_idx(p, s):
        return jnp.where(p < n_pairs, kblock_of(s), nk - 1), 0

    def ln_idx(p, s):
        tile = 2 * p + half_of(s) - 1
        idx = tile * nk + kblock_of(s)
        return jnp.clip(jnp.where(tile < 0, 0, idx), 0, n_tiles * nk - 1), 0

    def body(a_ref, w_ref, r_ref, g_ref, b_ref, o32_ref, o16_ref, acc0, acc1):
        p = pl.program_id(0)
        s = pl.program_id(1)
        active = p < n_pairs

        def product():
            return jnp.dot(a_ref[...], w_ref[...], preferred_element_type=F32)

        def normalise(acc_src, kb):
            g = g_ref[...]
            b = b_ref[...]
            base = pl.multiple_of(kb * rc, rc)
            for grp in range(rc // LN_ROWS):
                src = pl.ds(base + grp * LN_ROWS, LN_ROWS)
                dst = pl.ds(grp * LN_ROWS, LN_ROWS)
                y = alpha * r_ref[dst, :] + acc_src[src, :]
                mu = jnp.mean(y, axis=-1, keepdims=True)
                d = y - mu
                var = jnp.mean(d * d, axis=-1, keepdims=True)
                out = d * lax.rsqrt(var + EPS) * g + b
                o32_ref[dst, :] = out
                o16_ref[dst, :] = out.astype(BF16)

        first_pair = p == 0
        later_pair = jnp.logical_and(active, jnp.logical_not(first_pair))
        mid_first_half = jnp.logical_and(s > 0, s < nk)

        @pl.when(jnp.logical_and(s == 0, first_pair))
        def _():
            acc0[...] = product()

        @pl.when(jnp.logical_and(mid_first_half, first_pair))
        def _():
            acc0[...] += product()

        @pl.when(jnp.logical_and(s == 0, later_pair))
        def _():
            acc0[...] = product()
            normalise(acc1, 0)

        @pl.when(jnp.logical_and(mid_first_half, later_pair))
        def _():
            acc0[...] += product()
            normalise(acc1, s)

        @pl.when(jnp.logical_and(s == nk, active))
        def _():
            acc1[...] = product()
            normalise(acc0, 0)

        @pl.when(jnp.logical_and(s > nk, active))
        def _():
            acc1[...] += product()
            normalise(acc0, s - nk)

        @pl.when(jnp.logical_and(s < nk, jnp.logical_not(active)))
        def _():
            normalise(acc1, s)

    vec = pl.BlockSpec((1, n), lambda p, s: (0, 0))
    ln_rows_spec = pl.BlockSpec((rc, n), ln_idx)
    block_bytes = _nbytes((bm, bk), BF16) + _nbytes((bk, n), BF16) \
        + 2 * _nbytes((rc, n), F32) + _nbytes((rc, n), BF16)
    return pl.pallas_call(
        body,
        out_shape=[jax.ShapeDtypeStruct((m, n), F32), jax.ShapeDtypeStruct((m, n), BF16)],
        grid=grid,
        in_specs=[pl.BlockSpec((bm, bk), a_idx), pl.BlockSpec((bk, n), w_idx),
                  ln_rows_spec, vec, vec],
        out_specs=[ln_rows_spec, ln_rows_spec],
        scratch_shapes=[pltpu.VMEM((bm, n), F32), pltpu.VMEM((bm, n), F32)],
        compiler_params=_params(("arbitrary", "arbitrary"), block_bytes,
                                scratch_bytes=2 * bm * n * 4, temp_bytes=2 * 1024 * 1024),
        name=name,
    )(a, w, resid, gamma, beta)


def _retention_body(q_ref, k_ref, v_ref, g_ref, d_ref, qd_ref, kd_ref, gain_ref, o_ref):
    seq = q_ref.shape[0]
    blk = d_ref.shape[0]
    dmat = d_ref[...]
    qdec = qd_ref[...]
    kdec = kd_ref[...]
    block_decay = qdec[blk - 1:blk, :]
    gain = gain_ref[...]
    state = jnp.zeros((DK_A, DK_A), F32)
    for n in range(seq // blk):
        rows = pl.ds(n * blk, blk)
        q = q_ref[rows, :]
        k = k_ref[rows, :]
        v = v_ref[rows, :]
        s = lax.dot_general(q, k, (((1,), (1,)), ((), ())), preferred_element_type=F32)
        o = jnp.dot((s * dmat).astype(BF16), v, preferred_element_type=F32)
        q_in = (q.astype(F32) * qdec).astype(BF16)
        o = o + jnp.dot(q_in, state.astype(BF16), preferred_element_type=F32)
        k_out_t = (k.astype(F32) * kdec).T.astype(BF16)
        state = state * block_decay + jnp.dot(k_out_t, v, preferred_element_type=F32)
        mu = jnp.mean(o, axis=-1, keepdims=True)
        d = o - mu
        var = jnp.mean(d * d, axis=-1, keepdims=True)
        y = d * lax.rsqrt(var + EPS) * gain
        o_ref[rows, :] = (g_ref[rows, :].astype(F32) * y).astype(o_ref.dtype)


def _rglru_body(x_ref, gg_ref, cw_ref, cb_ref, wa_ref, ba_ref, wx_ref, bx_ref, lam_ref, o_ref,
                xpad, a_s, u_s, h_s, before_scan=None):
    nb, tt, _ = x_ref.shape
    halo = (CONV_K - 1) * nb
    t = pl.program_id(1)

    @pl.when(t == 0)
    def _():
        xpad[0:halo, :] = jnp.zeros((halo, BW_B), F32)
        h_s[...] = jnp.zeros_like(h_s)

    xpad[halo:, :] = jnp.swapaxes(x_ref[...], 0, 1).reshape(tt * nb, BW_B)
    cw = cw_ref[...]
    xc2 = cb_ref[...]
    for tap in range(CONV_K):
        xc2 = xc2 + xpad[tap * nb:(tap + tt) * nb, :] * cw[tap:tap + 1, :]
    xpad[0:halo, :] = xpad[tt * nb:tt * nb + halo, :]

    xb = xc2.astype(BF16)
    r = _sigmoid(jnp.dot(xb, wa_ref[...].astype(BF16), preferred_element_type=F32) + ba_ref[...])
    ig = _sigmoid(jnp.dot(xb, wx_ref[...].astype(BF16), preferred_element_type=F32) + bx_ref[...])
    neg_lam = -lam_ref[...]
    softplus = jnp.maximum(neg_lam, 0.0) + jnp.log(1.0 + jnp.exp(-jnp.abs(neg_lam)))
    a = jnp.exp((-C_RG * softplus) * r)
    a_s[...] = a
    u_s[...] = jnp.sqrt(1.0 - a * a) * (ig * xc2)
    if before_scan is not None:
        before_scan()

    def step(s, h):
        rows = pl.ds(pl.multiple_of(s * nb, nb), nb)
        h = a_s[rows, :] * h + u_s[rows, :]
        u_s[rows, :] = h
        return h

    h_s[...] = lax.fori_loop(0, tt, step, h_s[...], unroll=8)
    hs = jnp.swapaxes(u_s[...].reshape(tt, nb, BW_B), 0, 1)
    o_ref[...] = (hs * gg_ref[...]).astype(o_ref.dtype)


N_RGLRU_IN = 9
N_RET_IN = 8


def _mixers_body(*refs):
    rg_in = refs[:N_RGLRU_IN]
    ret_in = refs[N_RGLRU_IN:N_RGLRU_IN + N_RET_IN]
    rec_ref, ret_ref = refs[N_RGLRU_IN + N_RET_IN:N_RGLRU_IN + N_RET_IN + 2]
    scratch = refs[N_RGLRU_IN + N_RET_IN + 2:]
    _rglru_body(*rg_in, rec_ref, *scratch,
                before_scan=lambda: _retention_body(*ret_in, ret_ref))


def _mixers(q, k, v, g, dmat, qdec, kdec, gain, xr, gg, conv_w, conv_b, wa, ba, wx, bx, lam,
            batch, seq):
    assert H_A == H_B and seq % batch == 0
    tt = seq // batch
    blk = dmat.shape[1]
    xr3 = xr.reshape(batch, seq, W_B)
    gg3 = gg.reshape(batch, seq, W_B)
    tile = pl.BlockSpec((batch, tt, BW_B), lambda c, j: (0, j, c))
    vec = pl.BlockSpec((1, BW_B), lambda c, j: (0, c))
    gate_w = pl.BlockSpec((None, BW_B, BW_B), lambda c, j: (c, 0, 0))
    seq_head = pl.BlockSpec((seq, DK_A), lambda c, j: (j, c))
    head = lambda c, j: (c, 0, 0)
    block_bytes = 2 * _nbytes((batch, tt, BW_B), F32) + _nbytes((batch, tt, BW_B), BF16) \
        + 2 * _nbytes((BW_B, BW_B), F32) + 5 * _nbytes((seq, DK_A), BF16) \
        + _nbytes((blk, blk), F32) + 2 * _nbytes((blk, DK_A), F32)
    scratch = [pltpu.VMEM(((tt + CONV_K - 1) * batch, BW_B), F32),
               pltpu.VMEM((batch * tt, BW_B), F32),
               pltpu.VMEM((batch * tt, BW_B), F32),
               pltpu.VMEM((batch, BW_B), F32)]
    rec, ret = pl.pallas_call(
        _mixers_body,
        out_shape=[jax.ShapeDtypeStruct((batch, seq, W_B), BF16),
                   jax.ShapeDtypeStruct((batch * seq, W_A), BF16)],
        grid=(H_B, batch),
        in_specs=[tile, tile, pl.BlockSpec((CONV_K, BW_B), lambda c, j: (0, c)),
                  vec, gate_w, vec, gate_w, vec, vec,
                  seq_head, seq_head, seq_head, seq_head,
                  pl.BlockSpec((None, blk, blk), head),
                  pl.BlockSpec((None, blk, DK_A), head),
                  pl.BlockSpec((None, blk, DK_A), head),
                  pl.BlockSpec((1, DK_A), lambda c, j: (0, c))],
        out_specs=[tile, seq_head],
        scratch_shapes=scratch,
        compiler_params=_params(("parallel", "arbitrary"), block_bytes,
                                scratch_bytes=3 * _nbytes((batch * (tt + CONV_K), BW_B), F32),
                                temp_bytes=16 * batch * tt * BW_B * 4),
        name="mixers",
    )(xr3, gg3, conv_w, conv_b, wa, ba, wx, bx, lam, q, k, v, g, dmat, qdec, kdec, gain)
    return ret, rec.reshape(batch * seq, W_B)


def _retention_tables(blk):
    lg = jnp.log1p(-jnp.exp2(-5.0 - jnp.arange(H_A, dtype=F32)))
    idx = jnp.arange(blk, dtype=F32)
    diff = jnp.abs(idx[:, None] - idx[None, :])
    chunk = jnp.arange(blk) // CHUNK
    visible = chunk[None, :] <= chunk[:, None]
    dmat = jnp.where(visible[None], jnp.exp(lg[:, None, None] * diff[None]), 0.0)
    qdec = jnp.exp(lg[:, None] * (idx + 1.0))
    kdec = jnp.exp(lg[:, None] * (blk - 1.0 - idx))
    bcast = lambda t: jnp.broadcast_to(t[:, :, None], (H_A, blk, DK_A))
    return dmat.astype(F32), bcast(qdec), bcast(kdec)


def _rotary_tables(seq):
    pos = jnp.arange(seq, dtype=F32)
    inv_freq = ROPE_BASE ** (-jnp.arange(0, DK_A, 2, dtype=F32) / DK_A)
    ang = pos[:, None] * inv_freq[None, :]
    cos, sin = jnp.cos(ang), jnp.sin(ang)
    return jnp.concatenate([cos, cos], axis=-1), jnp.concatenate([-sin, sin], axis=-1)


def kernel(x, w_in, ret_gn_w, conv_w, conv_b, rg_wa, rg_ba, rg_wx, rg_bx, rg_lambda, w_br_a, w_br_b,
           b_gate, w_o, ln1_g, ln1_b, w_up, w_down, ln2_g, ln2_b):
    batch, seq, d_model = x.shape
    depth = w_in.shape[0]
    d_ff = w_up.shape[-1]
    m = batch * seq
    alpha = (2.0 * depth) ** 0.25
    assert w_in.shape[-1] == 4 * W_A + 2 * W_B + 2 * d_model
    assert batch == SUBLANES, "the scan packs the batch onto the sublanes of one vreg"

    blk = _tile(seq, RET_BLOCK)
    dmat, qdec, kdec = _retention_tables(blk)
    cos2, sin2 = _rotary_tables(seq)

    bm = _tile(seq, 1024)
    bn = 1024
    assert W_A == bn and W_B == bn
    pos_blocks = seq // bm
    pos_map = lambda i, j: (i % pos_blocks, 0)
    k_scale = DK_A ** -0.5

    def rotary_epilogue(scale):
        def epilogue(accs, e_refs, o_ref, cols):
            cos_ref, sin_ref = e_refs
            c, s = cos_ref[...], sin_ref[...]
            acc = accs[0]
            for hh in range(acc.shape[1] // DK_A):
                t = acc[:, hh * DK_A:(hh + 1) * DK_A]
                rot = t * c + pltpu.roll(t, DK_A // 2, 1) * s
                if scale is not None:
                    rot = rot * scale
                lo = cols.start + hh * DK_A
                o_ref[:, lo:lo + DK_A] = rot.astype(o_ref.dtype)
        return epilogue

    def cast_epilogue(accs, e_refs, o_ref, cols):
        o_ref[:, cols] = accs[0].astype(o_ref.dtype)

    def silu_epilogue(accs, e_refs, o_ref, cols):
        o_ref[:, cols] = (accs[0] * _sigmoid(accs[0])).astype(o_ref.dtype)

    def gelu_epilogue(accs, e_refs, o_ref, cols):
        o_ref[:, cols] = _gelu_tanh(accs[0])

    def merge_epilogue(accs, e_refs, o_ref, cols):
        ga, gb, ya, yb = accs
        ba_ref, bb_ref = e_refs
        out = _sigmoid(ga + ba_ref[:, cols]) * ya + _sigmoid(gb + bb_ref[:, cols]) * yb
        o_ref[:, cols] = out.astype(o_ref.dtype)

    def relu2_epilogue(accs, e_refs, o_ref, cols):
        h = jnp.maximum(accs[0], 0.0)
        o_ref[:, cols] = (h * h).astype(o_ref.dtype)

    x32 = x.reshape(m, d_model)
    x16 = x32.astype(BF16)
    w_in_b = w_in[0].astype(BF16)
    bn_merge = 512
    bm_merge = _tile(m, 512)
    bm_ln = _tile(m, 512)
    gate_a_off = (4 * W_A + 2 * W_B) // bn_merge
    gate_b_off = gate_a_off + d_model // bn_merge
    rot_extras = [(cos2, (bm, DK_A), pos_map), (sin2, (bm, DK_A), pos_map)]

    def in_proj(name, col_block, extras, epilogue, dtype, casts=()):
        return _fused_matmul(name, [x16], [(0, w_in_b, col_block)], extras, epilogue, bn, dtype,
                             bm, bn, rhs_resident=True, n_chunks=2, casts=casts)

    for l in range(depth):
        q, w_br_a_b = in_proj("in_proj_q", 0, rot_extras, rotary_epilogue(None), BF16, [(w_br_a, l)])
        k, w_br_b_b = in_proj("in_proj_k", 1, rot_extras, rotary_epilogue(k_scale), BF16, [(w_br_b, l)])
        v, = in_proj("in_proj_v", 2, [], cast_epilogue, BF16)
        g, = in_proj("in_proj_g", 3, [], silu_epilogue, BF16)
        xr, = in_proj("in_proj_xr", 4, [], cast_epilogue, F32)
        gg, = in_proj("in_proj_gr", 5, [], gelu_epilogue, F32)

        ret, rec = _mixers(q, k, v, g, dmat, qdec, kdec, ret_gn_w[l].reshape(1, W_A),
                           xr, gg, conv_w[l], conv_b[l].reshape(1, W_B), rg_wa[l],
                           rg_ba[l].reshape(1, W_B), rg_wx[l], rg_bx[l].reshape(1, W_B),
                           rg_lambda[l].reshape(1, W_B), batch, seq)

        bg = b_gate[l].reshape(1, 2 * d_model)
        merged, w_up_b, w_o_b = _fused_matmul(
            "merge", [x16, ret, rec],
            [(0, w_in_b, gate_a_off), (0, w_in_b, gate_b_off), (1, w_br_a_b, 0), (2, w_br_b_b, 0)],
            [(bg, (1, bn_merge), lambda i, j: (0, j)),
             (bg, (1, bn_merge), lambda i, j: (0, d_model // bn_merge + j))],
            merge_epilogue, d_model, BF16, bm_merge, bn_merge, rhs_resident=True, n_chunks=2,
            casts=[(w_up, l), (w_o, l)])

        x32, x16 = _matmul_residual_ln(
            "out_proj_ln", merged, w_o_b, x32, alpha, ln1_g[l].reshape(1, d_model),
            ln1_b[l].reshape(1, d_model), bm_ln, _tile(d_model, 1024))

        next_w_in = [(w_in, l + 1)] if l + 1 < depth else []
        hid, w_down_b, *w_in_next = _fused_matmul(
            "mlp_up", [x16], [(0, w_up_b, 0)], [], relu2_epilogue, d_ff, BF16, bm, bn,
            rhs_resident=False, n_chunks=2, casts=[(w_down, l)] + next_w_in)

        x32, x16 = _matmul_residual_ln(
            "mlp_down_ln", hid, w_down_b, x32, alpha, ln2_g[l].reshape(1, d_model),
            ln2_b[l].reshape(1, d_model), bm_ln, _tile(d_ff, 2048))
        if w_in_next:
            w_in_b = w_in_next[0]

    return x32.reshape(batch, seq, d_model)
```

```python
import math

import jax
import jax.numpy as jnp
from jax import lax
from jax.experimental import pallas as pl
from jax.experimental.pallas import tpu as pltpu

F32 = jnp.float32
BF16 = jnp.bfloat16

H_A = 8
DK_A = 128
W_A = H_A * DK_A
H_B = 8
BW_B = 128
W_B = H_B * BW_B
CONV_K = 4
C_RG = 8.0
CHUNK = 64
EPS = 1e-5
ROPE_BASE = 10000.0

V7X_VMEM_REQUEST_CAP = 60 * 1024 * 1024
LANES = 128
SUBLANES = 8
BF16_SUBLANES = 16

RET_BLOCK = 256
LN_ROWS = 16


def _nbytes(shape, dtype):
    return math.prod(shape) * jnp.dtype(dtype).itemsize


def _params(semantics, block_bytes, scratch_bytes=0, temp_bytes=0):
    need = 2 * block_bytes + scratch_bytes + temp_bytes + 4 * 1024 * 1024
    limit = min(max(need, 32 * 1024 * 1024), V7X_VMEM_REQUEST_CAP)
    return pltpu.CompilerParams(dimension_semantics=semantics, vmem_limit_bytes=limit)


def _tile(dim, pref):
    t = min(dim, pref)
    while dim % t:
        t //= 2
    return t


def _cast_plan(rows, cols, nsteps):
    ncb = 1
    while ncb <= nsteps:
        if cols % (ncb * LANES) == 0 and nsteps % ncb == 0:
            nrb = nsteps // ncb
            if rows % nrb == 0 and (rows // nrb) % BF16_SUBLANES == 0:
                return rows // nrb, cols // ncb
        ncb *= 2
    raise ValueError(f"no cast blocking for {(rows, cols)} in {nsteps} steps")


def _cast_specs(casts, nsteps, step_of):
    in_specs, out_specs, out_shapes, nbytes = [], [], [], 0
    for (w, lead) in casts:
        _, rows, cols = w.shape
        rb, cb = _cast_plan(rows, cols, nsteps)
        ncb = cols // cb

        def blk(*g, ncb=ncb):
            lin = step_of(*g)
            return lin // ncb, lin % ncb

        in_specs.append(pl.BlockSpec((None, rb, cb), lambda *g, lead=lead, blk=blk: (lead,) + blk(*g)))
        out_specs.append(pl.BlockSpec((rb, cb), lambda *g, blk=blk: blk(*g)))
        out_shapes.append(jax.ShapeDtypeStruct((rows, cols), BF16))
        nbytes += _nbytes((rb, cb), F32) + _nbytes((rb, cb), BF16)
    return in_specs, out_specs, out_shapes, nbytes


def _fused_matmul(name, lhs, dots, extras, epilogue, n_out, out_dtype, bm, bn, rhs_resident,
                  n_chunks=1, casts=()):
    m = lhs[0].shape[0]
    assert m % bm == 0 and n_out % bn == 0 and bn % n_chunks == 0
    if rhs_resident:
        grid = (n_out // bn, m // bm)
        ij = lambda g0, g1: (g1, g0)
    else:
        grid = (m // bm, n_out // bn)
        ij = lambda g0, g1: (g0, g1)

    in_specs, args, block_bytes = [], [], 0
    for a in lhs:
        k = a.shape[1]
        in_specs.append(pl.BlockSpec((bm, k), lambda g0, g1: (ij(g0, g1)[0], 0)))
        args.append(a)
        block_bytes += _nbytes((bm, k), a.dtype)
    for (_, w, off) in dots:
        k = w.shape[0]
        in_specs.append(pl.BlockSpec((k, bn), lambda g0, g1, off=off: (0, off + ij(g0, g1)[1])))
        args.append(w)
        block_bytes += _nbytes((k, bn), w.dtype)
    for (e, bshape, imap) in extras:
        in_specs.append(pl.BlockSpec(bshape, lambda g0, g1, imap=imap: imap(*ij(g0, g1))))
        args.append(e)
        block_bytes += _nbytes(bshape, e.dtype)
    block_bytes += _nbytes((bm, bn), out_dtype)
    c_in, c_out, c_shapes, c_bytes = _cast_specs(casts, grid[0] * grid[1],
                                                  lambda g0, g1: g0 * grid[1] + g1)
    in_specs += c_in
    args += [w for (w, _) in casts]
    block_bytes += c_bytes

    n_lhs, n_dots, n_extras, n_casts = len(lhs), len(dots), len(extras), len(casts)
    cw = bn // n_chunks

    def body(*refs):
        a_refs = refs[:n_lhs]
        w_refs = refs[n_lhs:n_lhs + n_dots]
        e_refs = refs[n_lhs + n_dots:n_lhs + n_dots + n_extras]
        ci_refs = refs[n_lhs + n_dots + n_extras:n_lhs + n_dots + n_extras + n_casts]
        o_ref = refs[n_lhs + n_dots + n_extras + n_casts]
        co_refs = refs[n_lhs + n_dots + n_extras + n_casts + 1:]
        def step(epi):
            for ci, co in zip(ci_refs, co_refs):
                co[...] = ci[...].astype(BF16)
            for c in range(n_chunks):
                cols = slice(c * cw, (c + 1) * cw)
                accs = [jnp.dot(a_refs[d[0]][...], w_ref[:, cols], preferred_element_type=F32)
                        for d, w_ref in zip(dots, w_refs)]
                epi(accs, e_refs, o_ref, cols)

        if callable(epilogue):
            step(epilogue)
        else:
            assert len(epilogue) == n_out // bn
            j = ij(pl.program_id(0), pl.program_id(1))[1]
            for cj, epi in enumerate(epilogue):
                pl.when(j == cj)(lambda epi=epi: step(epi))

    outs = pl.pallas_call(
        body,
        out_shape=[jax.ShapeDtypeStruct((m, n_out), out_dtype)] + c_shapes,
        grid=grid,
        in_specs=in_specs,
        out_specs=[pl.BlockSpec((bm, bn), lambda g0, g1: ij(g0, g1))] + c_out,
        compiler_params=_params(("parallel", "parallel"), block_bytes,
                                temp_bytes=(n_dots + 1) * bm * cw * 4),
        name=name,
    )(*args)
    return outs


def _gelu_tanh(x):
    c = math.sqrt(2.0 / math.pi)
    return 0.5 * x * (1.0 + jnp.tanh(c * (x + 0.044715 * (x * x * x))))


def _sigmoid(x):
    return 0.5 * jnp.tanh(0.5 * x) + 0.5


def _matmul_residual_ln(name, a, w, resid, alpha, gamma, beta, bm, bk):
    m, kdim = a.shape
    n = w.shape[1]
    nk = kdim // bk
    rc = bm // nk
    n_tiles = m // bm
    n_pairs = n_tiles // 2
    assert m % bm == 0 and kdim % bk == 0 and n_tiles % 2 == 0 and bm % nk == 0 and rc % LN_ROWS == 0
    grid = (n_pairs + 1, 2 * nk)

    half_of = lambda s: jnp.where(s >= nk, 1, 0)
    kblock_of = lambda s: s - nk * half_of(s)

    def a_idx(p, s):
        tile = jnp.minimum(2 * p + half_of(s), n_tiles - 1)
        return tile, jnp.where(p < n_pairs, kblock_of(s), nk - 1)

    def w_idx(p, s):
        return jnp.where(p < n_pairs, kblock_of(s), nk - 1), 0

    def ln_idx(p, s):
        tile = 2 * p + half_of(s) - 1
        idx = tile * nk + kblock_of(s)
        return jnp.clip(jnp.where(tile < 0, 0, idx), 0, n_tiles * nk - 1), 0

    def body(a_ref, w_ref, r_ref, g_ref, b_ref, o32_ref, o16_ref, acc0, acc1):
        p = pl.program_id(0)
        s = pl.program_id(1)
        active = p < n_pairs

        def product():
            return jnp.dot(a_ref[...], w_ref[...], preferred_element_type=F32)

        def normalise(acc_src, kb):
            g = g_ref[...]
            b = b_ref[...]
            base = pl.multiple_of(kb * rc, rc)
            for grp in range(rc // LN_ROWS):
                src = pl.ds(base + grp * LN_ROWS, LN_ROWS)
                dst = pl.ds(grp * LN_ROWS, LN_ROWS)
                y = alpha * r_ref[dst, :] + acc_src[src, :]
                mu = jnp.mean(y, axis=-1, keepdims=True)
                d = y - mu
                var = jnp.mean(d * d, axis=-1, keepdims=True)
                out = d * lax.rsqrt(var + EPS) * g + b
                o32_ref[dst, :] = out
                o16_ref[dst, :] = out.astype(BF16)

        first_pair = p == 0
        later_pair = jnp.logical_and(active, jnp.logical_not(first_pair))
        mid_first_half = jnp.logical_and(s > 0, s < nk)

        @pl.when(jnp.logical_and(s == 0, first_pair))
        def _():
            acc0[...] = product()

        @pl.when(jnp.logical_and(mid_first_half, first_pair))
        def _():
            acc0[...] += product()

        @pl.when(jnp.logical_and(s == 0, later_pair))
        def _():
            acc0[...] = product()
            normalise(acc1, 0)

        @pl.when(jnp.logical_and(mid_first_half, later_pair))
        def _():
            acc0[...] += product()
            normalise(acc1, s)

        @pl.when(jnp.logical_and(s == nk, active))
        def _():
            acc1[...] = product()
            normalise(acc0, 0)

        @pl.when(jnp.logical_and(s > nk, active))
        def _():
            acc1[...] += product()
            normalise(acc0, s - nk)

        @pl.when(jnp.logical_and(s < nk, jnp.logical_not(active)))
        def _():
            normalise(acc1, s)

    vec = pl.BlockSpec((1, n), lambda p, s: (0, 0))
    ln_rows_spec = pl.BlockSpec((rc, n), ln_idx)
    block_bytes = _nbytes((bm, bk), BF16) + _nbytes((bk, n), BF16) \
        + 2 * _nbytes((rc, n), F32) + _nbytes((rc, n), BF16)
    return pl.pallas_call(
        body,
        out_shape=[jax.ShapeDtypeStruct((m, n), F32), jax.ShapeDtypeStruct((m, n), BF16)],
        grid=grid,
        in_specs=[pl.BlockSpec((bm, bk), a_idx), pl.BlockSpec((bk, n), w_idx),
                  ln_rows_spec, vec, vec],
        out_specs=[ln_rows_spec, ln_rows_spec],
        scratch_shapes=[pltpu.VMEM((bm, n), F32), pltpu.VMEM((bm, n), F32)],
        compiler_params=_params(("arbitrary", "arbitrary"), block_bytes,
                                scratch_bytes=2 * bm * n * 4, temp_bytes=2 * 1024 * 1024),
        name=name,
    )(a, w, resid, gamma, beta)


def _retention_body(q_ref, k_ref, v_ref, g_ref, d_ref, qd_ref, kd_ref, gain_ref, o_ref):
    seq = q_ref.shape[0]
    blk = d_ref.shape[0]
    dmat = d_ref[...]
    qdec = qd_ref[...]
    kdec = kd_ref[...]
    block_decay = qdec[blk - 1:blk, :]
    gain = gain_ref[...]
    state = jnp.zeros((DK_A, DK_A), F32)
    for n in range(seq // blk):
        rows = pl.ds(n * blk, blk)
        q = q_ref[rows, :]
        k = k_ref[rows, :]
        v = v_ref[rows, :]
        s = lax.dot_general(q, k, (((1,), (1,)), ((), ())), preferred_element_type=F32)
        o = jnp.dot((s * dmat).astype(BF16), v, preferred_element_type=F32)
        q_in = (q.astype(F32) * qdec).astype(BF16)
        o = o + jnp.dot(q_in, state.astype(BF16), preferred_element_type=F32)
        k_out_t = (k.astype(F32) * kdec).T.astype(BF16)
        state = state * block_decay + jnp.dot(k_out_t, v, preferred_element_type=F32)
        mu = jnp.mean(o, axis=-1, keepdims=True)
        d = o - mu
        var = jnp.mean(d * d, axis=-1, keepdims=True)
        y = d * lax.rsqrt(var + EPS) * gain
        o_ref[rows, :] = (g_ref[rows, :].astype(F32) * y).astype(o_ref.dtype)


def _rglru_body(x_ref, gg_ref, cw_ref, cb_ref, wa_ref, ba_ref, wx_ref, bx_ref, lam_ref, o_ref,
                xpad, a_s, u_s, h_s, before_scan=None):
    nb, tt, _ = x_ref.shape
    halo = (CONV_K - 1) * nb
    t = pl.program_id(1)

    @pl.when(t == 0)
    def _():
        xpad[0:halo, :] = jnp.zeros((halo, BW_B), F32)
        h_s[...] = jnp.zeros_like(h_s)

    xpad[halo:, :] = jnp.swapaxes(x_ref[...], 0, 1).reshape(tt * nb, BW_B)
    cw = cw_ref[...]
    xc2 = cb_ref[...]
    for tap in range(CONV_K):
        xc2 = xc2 + xpad[tap * nb:(tap + tt) * nb, :] * cw[tap:tap + 1, :]
    xpad[0:halo, :] = xpad[tt * nb:tt * nb + halo, :]

    xb = xc2.astype(BF16)
    r = _sigmoid(jnp.dot(xb, wa_ref[...].astype(BF16), preferred_element_type=F32) + ba_ref[...])
    ig = _sigmoid(jnp.dot(xb, wx_ref[...].astype(BF16), preferred_element_type=F32) + bx_ref[...])
    neg_lam = -lam_ref[...]
    softplus = jnp.maximum(neg_lam, 0.0) + jnp.log(1.0 + jnp.exp(-jnp.abs(neg_lam)))
    a = jnp.exp((-C_RG * softplus) * r)
    a_s[...] = a
    u_s[...] = jnp.sqrt(1.0 - a * a) * (ig * xc2)
    if before_scan is not None:
        before_scan()

    def step(s, h):
        rows = pl.ds(pl.multiple_of(s * nb, nb), nb)
        h = a_s[rows, :] * h + u_s[rows, :]
        u_s[rows, :] = h
        return h

    h_s[...] = lax.fori_loop(0, tt, step, h_s[...], unroll=8)
    hs = jnp.swapaxes(u_s[...].reshape(tt, nb, BW_B), 0, 1)
    o_ref[...] = (hs * gg_ref[...]).astype(o_ref.dtype)


N_RGLRU_IN = 9
N_RET_IN = 8


def _mixers_body(*refs):
    rg_in = refs[:N_RGLRU_IN]
    ret_in = refs[N_RGLRU_IN:N_RGLRU_IN + N_RET_IN]
    rec_ref, ret_ref = refs[N_RGLRU_IN + N_RET_IN:N_RGLRU_IN + N_RET_IN + 2]
    scratch = refs[N_RGLRU_IN + N_RET_IN + 2:]
    _rglru_body(*rg_in, rec_ref, *scratch,
                before_scan=lambda: _retention_body(*ret_in, ret_ref))


def _mixers(qkvg, dmat, qdec, kdec, gain, xg, conv_w, conv_b, wa, ba, wx, bx, lam, batch, seq):
    assert H_A == H_B and seq % batch == 0
    tt = seq // batch
    blk = dmat.shape[1]
    xg3 = xg.reshape(batch, seq, 2 * W_B)
    tile = pl.BlockSpec((batch, tt, BW_B), lambda c, j: (0, j, c))
    gate_tile = pl.BlockSpec((batch, tt, BW_B), lambda c, j: (0, j, H_B + c))
    vec = pl.BlockSpec((1, BW_B), lambda c, j: (0, c))
    gate_w = pl.BlockSpec((None, BW_B, BW_B), lambda c, j: (c, 0, 0))
    seq_head = pl.BlockSpec((seq, DK_A), lambda c, j: (j, c))
    col_group = lambda grp: pl.BlockSpec((seq, DK_A), lambda c, j: (j, grp * H_A + c))
    head = lambda c, j: (c, 0, 0)
    block_bytes = 2 * _nbytes((batch, tt, BW_B), F32) + _nbytes((batch, tt, BW_B), BF16) \
        + 2 * _nbytes((BW_B, BW_B), F32) + 5 * _nbytes((seq, DK_A), BF16) \
        + _nbytes((blk, blk), F32) + 2 * _nbytes((blk, DK_A), F32)
    scratch = [pltpu.VMEM(((tt + CONV_K - 1) * batch, BW_B), F32),
               pltpu.VMEM((batch * tt, BW_B), F32),
               pltpu.VMEM((batch * tt, BW_B), F32),
               pltpu.VMEM((batch, BW_B), F32)]
    rec, ret = pl.pallas_call(
        _mixers_body,
        out_shape=[jax.ShapeDtypeStruct((batch, seq, W_B), BF16),
                   jax.ShapeDtypeStruct((batch * seq, W_A), BF16)],
        grid=(H_B, batch),
        in_specs=[tile, gate_tile, pl.BlockSpec((CONV_K, BW_B), lambda c, j: (0, c)),
                  vec, gate_w, vec, gate_w, vec, vec,
                  col_group(0), col_group(1), col_group(2), col_group(3),
                  pl.BlockSpec((None, blk, blk), head),
                  pl.BlockSpec((None, blk, DK_A), head),
                  pl.BlockSpec((None, blk, DK_A), head),
                  pl.BlockSpec((1, DK_A), lambda c, j: (0, c))],
        out_specs=[tile, seq_head],
        scratch_shapes=scratch,
        compiler_params=_params(("parallel", "arbitrary"), block_bytes,
                                scratch_bytes=3 * _nbytes((batch * (tt + CONV_K), BW_B), F32),
                                temp_bytes=16 * batch * tt * BW_B * 4),
        name="mixers",
    )(xg3, xg3, conv_w, conv_b, wa, ba, wx, bx, lam, qkvg, qkvg, qkvg, qkvg, dmat, qdec, kdec, gain)
    return ret, rec.reshape(batch * seq, W_B)


def _retention_tables(blk):
    lg = jnp.log1p(-jnp.exp2(-5.0 - jnp.arange(H_A, dtype=F32)))
    idx = jnp.arange(blk, dtype=F32)
    diff = jnp.abs(idx[:, None] - idx[None, :])
    chunk = jnp.arange(blk) // CHUNK
    visible = chunk[None, :] <= chunk[:, None]
    dmat = jnp.where(visible[None], jnp.exp(lg[:, None, None] * diff[None]), 0.0)
    qdec = jnp.exp(lg[:, None] * (idx + 1.0))
    kdec = jnp.exp(lg[:, None] * (blk - 1.0 - idx))
    bcast = lambda t: jnp.broadcast_to(t[:, :, None], (H_A, blk, DK_A))
    return dmat.astype(F32), bcast(qdec), bcast(kdec)


def _rotary_tables(seq):
    pos = jnp.arange(seq, dtype=F32)
    inv_freq = ROPE_BASE ** (-jnp.arange(0, DK_A, 2, dtype=F32) / DK_A)
    ang = pos[:, None] * inv_freq[None, :]
    cos, sin = jnp.cos(ang), jnp.sin(ang)
    return jnp.concatenate([cos, cos], axis=-1), jnp.concatenate([-sin, sin], axis=-1)


def kernel(x, w_in, ret_gn_w, conv_w, conv_b, rg_wa, rg_ba, rg_wx, rg_bx, rg_lambda, w_br_a, w_br_b,
           b_gate, w_o, ln1_g, ln1_b, w_up, w_down, ln2_g, ln2_b):
    batch, seq, d_model = x.shape
    depth = w_in.shape[0]
    d_ff = w_up.shape[-1]
    m = batch * seq
    alpha = (2.0 * depth) ** 0.25
    assert w_in.shape[-1] == 4 * W_A + 2 * W_B + 2 * d_model
    assert batch == SUBLANES, "the scan packs the batch onto the sublanes of one vreg"

    blk = _tile(seq, RET_BLOCK)
    dmat, qdec, kdec = _retention_tables(blk)
    cos2, sin2 = _rotary_tables(seq)

    bm = _tile(seq, 1024)
    bn = 1024
    assert W_A == bn and W_B == bn
    pos_blocks = seq // bm
    pos_map = lambda i, j: (i % pos_blocks, 0)
    k_scale = DK_A ** -0.5

    def rotary_epilogue(scale):
        def epilogue(accs, e_refs, o_ref, cols):
            cos_ref, sin_ref = e_refs
            c, s = cos_ref[...], sin_ref[...]
            acc = accs[0]
            for hh in range(acc.shape[1] // DK_A):
                t = acc[:, hh * DK_A:(hh + 1) * DK_A]
                rot = t * c + pltpu.roll(t, DK_A // 2, 1) * s
                if scale is not None:
                    rot = rot * scale
                lo = cols.start + hh * DK_A
                o_ref[:, lo:lo + DK_A] = rot.astype(o_ref.dtype)
        return epilogue

    def cast_epilogue(accs, e_refs, o_ref, cols):
        o_ref[:, cols] = accs[0].astype(o_ref.dtype)

    def silu_epilogue(accs, e_refs, o_ref, cols):
        o_ref[:, cols] = (accs[0] * _sigmoid(accs[0])).astype(o_ref.dtype)

    def gelu_epilogue(accs, e_refs, o_ref, cols):
        o_ref[:, cols] = _gelu_tanh(accs[0])

    def merge_epilogue(accs, e_refs, o_ref, cols):
        ga, gb, ya, yb = accs
        ba_ref, bb_ref = e_refs
        out = _sigmoid(ga + ba_ref[:, cols]) * ya + _sigmoid(gb + bb_ref[:, cols]) * yb
        o_ref[:, cols] = out.astype(o_ref.dtype)

    def relu2_epilogue(accs, e_refs, o_ref, cols):
        h = jnp.maximum(accs[0], 0.0)
        o_ref[:, cols] = (h * h).astype(o_ref.dtype)

    x32 = x.reshape(m, d_model)
    x16 = x32.astype(BF16)
    w_in_b = w_in[0].astype(BF16)
    bn_merge = 512
    bm_merge = _tile(m, 512)
    bm_ln = _tile(m, 512)
    gate_a_off = (4 * W_A + 2 * W_B) // bn_merge
    gate_b_off = gate_a_off + d_model // bn_merge
    rot_extras = [(cos2, (bm, DK_A), pos_map), (sin2, (bm, DK_A), pos_map)]

    for l in range(depth):
        qkvg, w_br_a_b, w_br_b_b = _fused_matmul(
            "in_proj_qkvg", [x16], [(0, w_in_b, 0)], rot_extras,
            [rotary_epilogue(None), rotary_epilogue(k_scale), cast_epilogue, silu_epilogue],
            4 * W_A, BF16, bm, bn, rhs_resident=True, n_chunks=2,
            casts=[(w_br_a, l), (w_br_b, l)])
        xg, = _fused_matmul(
            "in_proj_xg", [x16], [(0, w_in_b, 4 * W_A // bn)], [],
            [cast_epilogue, gelu_epilogue], 2 * W_B, F32, bm, bn, rhs_resident=True, n_chunks=2)

        ret, rec = _mixers(qkvg, dmat, qdec, kdec, ret_gn_w[l].reshape(1, W_A),
                           xg, conv_w[l], conv_b[l].reshape(1, W_B), rg_wa[l],
                           rg_ba[l].reshape(1, W_B), rg_wx[l], rg_bx[l].reshape(1, W_B),
                           rg_lambda[l].reshape(1, W_B), batch, seq)

        bg = b_gate[l].reshape(1, 2 * d_model)
        merged, w_up_b, w_o_b = _fused_matmul(
            "merge", [x16, ret, rec],
            [(0, w_in_b, gate_a_off), (0, w_in_b, gate_b_off), (1, w_br_a_b, 0), (2, w_br_b_b, 0)],
            [(bg, (1, bn_merge), lambda i, j: (0, j)),
             (bg, (1, bn_merge), lambda i, j: (0, d_model // bn_merge + j))],
            merge_epilogue, d_model, BF16, bm_merge, bn_merge, rhs_resident=True, n_chunks=2,
            casts=[(w_up, l), (w_o, l)])

        x32, x16 = _matmul_residual_ln(
            "out_proj_ln", merged, w_o_b, x32, alpha, ln1_g[l].reshape(1, d_model),
            ln1_b[l].reshape(1, d_model), bm_ln, _tile(d_model, 1024))

        next_w_in = [(w_in, l + 1)] if l + 1 < depth else []
        hid, w_down_b, *w_in_next = _fused_matmul(
            "mlp_up", [x16], [(0, w_up_b, 0)], [], relu2_epilogue, d_ff, BF16, bm, bn,
            rhs_resident=False, n_chunks=2, casts=[(w_down, l)] + next_w_in)

        x32, x16 = _matmul_residual_ln(
            "mlp_down_ln", hid, w_down_b, x32, alpha, ln2_g[l].reshape(1, d_model),
            ln2_b[l].reshape(1, d_model), bm_ln, _tile(d_ff, 2048))
        if w_in_next:
            w_in_b = w_in_next[0]

    return x32.reshape(batch, seq, d_model)
```

```python
import math

import jax
import jax.numpy as jnp
from jax import lax
from jax.experimental import pallas as pl
from jax.experimental.pallas import tpu as pltpu

F32 = jnp.float32
BF16 = jnp.bfloat16

H_A = 8
DK_A = 128
W_A = H_A * DK_A
H_B = 8
BW_B = 128
W_B = H_B * BW_B
CONV_K = 4
C_RG = 8.0
CHUNK = 64
EPS = 1e-5
ROPE_BASE = 10000.0

V7X_VMEM_REQUEST_CAP = 60 * 1024 * 1024
LANES = 128
SUBLANES = 8
BF16_SUBLANES = 16

RET_BLOCK = 256
LN_ROWS = 16


def _nbytes(shape, dtype):
    return math.prod(shape) * jnp.dtype(dtype).itemsize


def _params(semantics, block_bytes, scratch_bytes=0, temp_bytes=0):
    need = 2 * block_bytes + scratch_bytes + temp_bytes + 4 * 1024 * 1024
    limit = min(max(need, 32 * 1024 * 1024), V7X_VMEM_REQUEST_CAP)
    return pltpu.CompilerParams(dimension_semantics=semantics, vmem_limit_bytes=limit)


def _tile(dim, pref):
    t = min(dim, pref)
    while dim % t:
        t //= 2
    return t


def _cast_plan(rows, cols, nsteps):
    ncb = 1
    while ncb <= nsteps:
        if cols % (ncb * LANES) == 0 and nsteps % ncb == 0:
            nrb = nsteps // ncb
            if rows % nrb == 0 and (rows // nrb) % BF16_SUBLANES == 0:
                return rows // nrb, cols // ncb
        ncb *= 2
    raise ValueError(f"no cast blocking for {(rows, cols)} in {nsteps} steps")


def _cast_specs(casts, nsteps, step_of):
    in_specs, out_specs, out_shapes, nbytes = [], [], [], 0
    for (w, lead) in casts:
        _, rows, cols = w.shape
        rb, cb = _cast_plan(rows, cols, nsteps)
        ncb = cols // cb

        def blk(*g, ncb=ncb):
            lin = step_of(*g)
            return lin // ncb, lin % ncb

        in_specs.append(pl.BlockSpec((None, rb, cb), lambda *g, lead=lead, blk=blk: (lead,) + blk(*g)))
        out_specs.append(pl.BlockSpec((rb, cb), lambda *g, blk=blk: blk(*g)))
        out_shapes.append(jax.ShapeDtypeStruct((rows, cols), BF16))
        nbytes += _nbytes((rb, cb), F32) + _nbytes((rb, cb), BF16)
    return in_specs, out_specs, out_shapes, nbytes


def _fused_matmul(name, lhs, dots, extras, epilogue, n_out, out_dtype, bm, bn, rhs_resident,
                  n_chunks=1, casts=()):
    m = lhs[0].shape[0]
    assert m % bm == 0 and n_out % bn == 0 and bn % n_chunks == 0
    if rhs_resident:
        grid = (n_out // bn, m // bm)
        ij = lambda g0, g1: (g1, g0)
    else:
        grid = (m // bm, n_out // bn)
        ij = lambda g0, g1: (g0, g1)

    in_specs, args, block_bytes = [], [], 0
    for a in lhs:
        k = a.shape[1]
        in_specs.append(pl.BlockSpec((bm, k), lambda g0, g1: (ij(g0, g1)[0], 0)))
        args.append(a)
        block_bytes += _nbytes((bm, k), a.dtype)
    for (_, w, off) in dots:
        k = w.shape[0]
        in_specs.append(pl.BlockSpec((k, bn), lambda g0, g1, off=off: (0, off + ij(g0, g1)[1])))
        args.append(w)
        block_bytes += _nbytes((k, bn), w.dtype)
    for (e, bshape, imap) in extras:
        in_specs.append(pl.BlockSpec(bshape, lambda g0, g1, imap=imap: imap(*ij(g0, g1))))
        args.append(e)
        block_bytes += _nbytes(bshape, e.dtype)
    block_bytes += _nbytes((bm, bn), out_dtype)
    c_in, c_out, c_shapes, c_bytes = _cast_specs(casts, grid[0] * grid[1],
                                                  lambda g0, g1: g0 * grid[1] + g1)
    in_specs += c_in
    args += [w for (w, _) in casts]
    block_bytes += c_bytes

    n_lhs, n_dots, n_extras, n_casts = len(lhs), len(dots), len(extras), len(casts)
    cw = bn // n_chunks

    def body(*refs):
        a_refs = refs[:n_lhs]
        w_refs = refs[n_lhs:n_lhs + n_dots]
        e_refs = refs[n_lhs + n_dots:n_lhs + n_dots + n_extras]
        ci_refs = refs[n_lhs + n_dots + n_extras:n_lhs + n_dots + n_extras + n_casts]
        o_ref = refs[n_lhs + n_dots + n_extras + n_casts]
        co_refs = refs[n_lhs + n_dots + n_extras + n_casts + 1:]
        def step(epi):
            for ci, co in zip(ci_refs, co_refs):
                co[...] = ci[...].astype(BF16)
            for c in range(n_chunks):
                cols = slice(c * cw, (c + 1) * cw)
                accs = [jnp.dot(a_refs[d[0]][...], w_ref[:, cols], preferred_element_type=F32)
                        for d, w_ref in zip(dots, w_refs)]
                epi(accs, e_refs, o_ref, cols)

        if callable(epilogue):
            step(epilogue)
        else:
            assert len(epilogue) == n_out // bn
            j = ij(pl.program_id(0), pl.program_id(1))[1]
            for cj, epi in enumerate(epilogue):
                pl.when(j == cj)(lambda epi=epi: step(epi))

    outs = pl.pallas_call(
        body,
        out_shape=[jax.ShapeDtypeStruct((m, n_out), out_dtype)] + c_shapes,
        grid=grid,
        in_specs=in_specs,
        out_specs=[pl.BlockSpec((bm, bn), lambda g0, g1: ij(g0, g1))] + c_out,
        compiler_params=_params(("parallel", "parallel"), block_bytes,
                                temp_bytes=(n_dots + 1) * bm * cw * 4),
        name=name,
    )(*args)
    return outs


def _gelu_tanh(x):
    c = math.sqrt(2.0 / math.pi)
    return 0.5 * x * (1.0 + jnp.tanh(c * (x + 0.044715 * (x * x * x))))


def _sigmoid(x):
    return 0.5 * jnp.tanh(0.5 * x) + 0.5


def _matmul_residual_ln(name, a, w, resid, alpha, gamma, beta, bm, bk):
    m, kdim = a.shape
    n = w.shape[1]
    nk = kdim // bk
    rc = bm // nk
    n_tiles = m // bm
    n_pairs = n_tiles // 2
    assert m % bm == 0 and kdim % bk == 0 and n_tiles % 2 == 0 and bm % nk == 0 and rc % LN_ROWS == 0
    grid = (n_pairs + 1, 2 * nk)

    half_of = lambda s: jnp.where(s >= nk, 1, 0)
    kblock_of = lambda s: s - nk * half_of(s)

    def a_idx(p, s):
        tile = jnp.minimum(2 * p + half_of(s), n_tiles - 1)
        return tile, jnp.where(p < n_pairs, kblock_of(s), nk - 1)

    def w_idx(p, s):
        return jnp.where(p < n_pairs, kblock_of(s), nk - 1), 0

    def ln_idx(p, s):
        tile = 2 * p + half_of(s) - 1
        idx = tile * nk + kblock_of(s)
        return jnp.clip(jnp.where(tile < 0, 0, idx), 0, n_tiles * nk - 1), 0

    def body(a_ref, w_ref, r_ref, g_ref, b_ref, o32_ref, o16_ref, acc0, acc1):
        p = pl.program_id(0)
        s = pl.program_id(1)
        active = p < n_pairs

        def product():
            return jnp.dot(a_ref[...], w_ref[...], preferred_element_type=F32)

        def normalise(acc_src, kb):
            g = g_ref[...]
            b = b_ref[...]
            base = pl.multiple_of(kb * rc, rc)
            for grp in range(rc // LN_ROWS):
                src = pl.ds(base + grp * LN_ROWS, LN_ROWS)
                dst = pl.ds(grp * LN_ROWS, LN_ROWS)
                y = alpha * r_ref[dst, :] + acc_src[src, :]
                mu = jnp.mean(y, axis=-1, keepdims=True)
                d = y - mu
                var = jnp.mean(d * d, axis=-1, keepdims=True)
                out = d * lax.rsqrt(var + EPS) * g + b
                o32_ref[dst, :] = out
                o16_ref[dst, :] = out.astype(BF16)

        first_pair = p == 0
        later_pair = jnp.logical_and(active, jnp.logical_not(first_pair))
        mid_first_half = jnp.logical_and(s > 0, s < nk)

        @pl.when(jnp.logical_and(s == 0, first_pair))
        def _():
            acc0[...] = product()

        @pl.when(jnp.logical_and(mid_first_half, first_pair))
        def _():
            acc0[...] += product()

        @pl.when(jnp.logical_and(s == 0, later_pair))
        def _():
            acc0[...] = product()
            normalise(acc1, 0)

        @pl.when(jnp.logical_and(mid_first_half, later_pair))
        def _():
            acc0[...] += product()
            normalise(acc1, s)

        @pl.when(jnp.logical_and(s == nk, active))
        def _():
            acc1[...] = product()
            normalise(acc0, 0)

        @pl.when(jnp.logical_and(s > nk, active))
        def _():
            acc1[...] += product()
            normalise(acc0, s - nk)

        @pl.when(jnp.logical_and(s < nk, jnp.logical_not(active)))
        def _():
            normalise(acc1, s)

    vec = pl.BlockSpec((1, n), lambda p, s: (0, 0))
    ln_rows_spec = pl.BlockSpec((rc, n), ln_idx)
    block_bytes = _nbytes((bm, bk), BF16) + _nbytes((bk, n), BF16) \
        + 2 * _nbytes((rc, n), F32) + _nbytes((rc, n), BF16)
    return pl.pallas_call(
        body,
        out_shape=[jax.ShapeDtypeStruct((m, n), F32), jax.ShapeDtypeStruct((m, n), BF16)],
        grid=grid,
        in_specs=[pl.BlockSpec((bm, bk), a_idx), pl.BlockSpec((bk, n), w_idx),
                  ln_rows_spec, vec, vec],
        out_specs=[ln_rows_spec, ln_rows_spec],
        scratch_shapes=[pltpu.VMEM((bm, n), F32), pltpu.VMEM((bm, n), F32)],
        compiler_params=_params(("arbitrary", "arbitrary"), block_bytes,
                                scratch_bytes=2 * bm * n * 4, temp_bytes=2 * 1024 * 1024),
        name=name,
    )(a, w, resid, gamma, beta)


def _retention_body(q_ref, k_ref, v_ref, g_ref, d_ref, qd_ref, kd_ref, gain_ref, o_ref):
    seq = q_ref.shape[0]
    blk = d_ref.shape[0]
    dmat = d_ref[...]
    qdec = qd_ref[...]
    kdec = kd_ref[...]
    block_decay = qdec[blk - 1:blk, :]
    gain = gain_ref[...]
    state = jnp.zeros((DK_A, DK_A), F32)
    for n in range(seq // blk):
        rows = pl.ds(n * blk, blk)
        q = q_ref[rows, :]
        k = k_ref[rows, :]
        v = v_ref[rows, :]
        s = lax.dot_general(q, k, (((1,), (1,)), ((), ())), preferred_element_type=F32)
        o = jnp.dot((s * dmat).astype(BF16), v, preferred_element_type=F32)
        q_in = (q.astype(F32) * qdec).astype(BF16)
        o = o + jnp.dot(q_in, state.astype(BF16), preferred_element_type=F32)
        k_out_t = (k.astype(F32) * kdec).T.astype(BF16)
        state = state * block_decay + jnp.dot(k_out_t, v, preferred_element_type=F32)
        mu = jnp.mean(o, axis=-1, keepdims=True)
        d = o - mu
        var = jnp.mean(d * d, axis=-1, keepdims=True)
        y = d * lax.rsqrt(var + EPS) * gain
        o_ref[rows, :] = (g_ref[rows, :].astype(F32) * y).astype(o_ref.dtype)


def _rglru_body(x_ref, gg_ref, cw_ref, cb_ref, wa_ref, ba_ref, wx_ref, bx_ref, lam_ref, o_ref,
                xpad, a_s, u_s, h_s, before_scan=None):
    nb, tt, _ = x_ref.shape
    halo = (CONV_K - 1) * nb
    t = pl.program_id(1)

    @pl.when(t == 0)
    def _():
        xpad[0:halo, :] = jnp.zeros((halo, BW_B), F32)
        h_s[...] = jnp.zeros_like(h_s)

    xpad[halo:, :] = jnp.swapaxes(x_ref[...], 0, 1).reshape(tt * nb, BW_B)
    cw = cw_ref[...]
    xc2 = cb_ref[...]
    for tap in range(CONV_K):
        xc2 = xc2 + xpad[tap * nb:(tap + tt) * nb, :] * cw[tap:tap + 1, :]
    xpad[0:halo, :] = xpad[tt * nb:tt * nb + halo, :]

    xb = xc2.astype(BF16)
    r = _sigmoid(jnp.dot(xb, wa_ref[...].astype(BF16), preferred_element_type=F32) + ba_ref[...])
    ig = _sigmoid(jnp.dot(xb, wx_ref[...].astype(BF16), preferred_element_type=F32) + bx_ref[...])
    neg_lam = -lam_ref[...]
    softplus = jnp.maximum(neg_lam, 0.0) + jnp.log(1.0 + jnp.exp(-jnp.abs(neg_lam)))
    a = jnp.exp((-C_RG * softplus) * r)
    a_s[...] = a
    u_s[...] = jnp.sqrt(1.0 - a * a) * (ig * xc2)
    if before_scan is not None:
        before_scan()

    def step(s, h):
        rows = pl.ds(pl.multiple_of(s * nb, nb), nb)
        h = a_s[rows, :] * h + u_s[rows, :]
        u_s[rows, :] = h
        return h

    h_s[...] = lax.fori_loop(0, tt, step, h_s[...], unroll=8)
    hs = jnp.swapaxes(u_s[...].reshape(tt, nb, BW_B), 0, 1)
    o_ref[...] = (hs * gg_ref[...]).astype(o_ref.dtype)


N_RGLRU_IN = 9
N_RET_IN = 8


def _mixers_body(*refs):
    rg_in = refs[:N_RGLRU_IN]
    ret_in = refs[N_RGLRU_IN:N_RGLRU_IN + N_RET_IN]
    rec_ref, ret_ref = refs[N_RGLRU_IN + N_RET_IN:N_RGLRU_IN + N_RET_IN + 2]
    scratch = refs[N_RGLRU_IN + N_RET_IN + 2:]
    _rglru_body(*rg_in, rec_ref, *scratch,
                before_scan=lambda: _retention_body(*ret_in, ret_ref))


def _mixers(qkvg, dmat, qdec, kdec, gain, xg, conv_w, conv_b, wa, ba, wx, bx, lam, batch, seq):
    assert H_A == H_B and seq % batch == 0
    tt = seq // batch
    blk = dmat.shape[1]
    xg3 = xg.reshape(batch, seq, 2 * W_B)
    tile = pl.BlockSpec((batch, tt, BW_B), lambda c, j: (0, j, c))
    gate_tile = pl.BlockSpec((batch, tt, BW_B), lambda c, j: (0, j, H_B + c))
    vec = pl.BlockSpec((1, BW_B), lambda c, j: (0, c))
    gate_w = pl.BlockSpec((None, BW_B, BW_B), lambda c, j: (c, 0, 0))
    seq_head = pl.BlockSpec((seq, DK_A), lambda c, j: (j, c))
    col_group = lambda grp: pl.BlockSpec((seq, DK_A), lambda c, j: (j, grp * H_A + c))
    head = lambda c, j: (c, 0, 0)
    block_bytes = 2 * _nbytes((batch, tt, BW_B), F32) + _nbytes((batch, tt, BW_B), BF16) \
        + 2 * _nbytes((BW_B, BW_B), F32) + 5 * _nbytes((seq, DK_A), BF16) \
        + _nbytes((blk, blk), F32) + 2 * _nbytes((blk, DK_A), F32)
    scratch = [pltpu.VMEM(((tt + CONV_K - 1) * batch, BW_B), F32),
               pltpu.VMEM((batch * tt, BW_B), F32),
               pltpu.VMEM((batch * tt, BW_B), F32),
               pltpu.VMEM((batch, BW_B), F32)]
    rec, ret = pl.pallas_call(
        _mixers_body,
        out_shape=[jax.ShapeDtypeStruct((batch, seq, W_B), BF16),
                   jax.ShapeDtypeStruct((batch * seq, W_A), BF16)],
        grid=(H_B, batch),
        in_specs=[tile, gate_tile, pl.BlockSpec((CONV_K, BW_B), lambda c, j: (0, c)),
                  vec, gate_w, vec, gate_w, vec, vec,
                  col_group(0), col_group(1), col_group(2), col_group(3),
                  pl.BlockSpec((None, blk, blk), head),
                  pl.BlockSpec((None, blk, DK_A), head),
                  pl.BlockSpec((None, blk, DK_A), head),
                  pl.BlockSpec((1, DK_A), lambda c, j: (0, c))],
        out_specs=[tile, seq_head],
        scratch_shapes=scratch,
        compiler_params=_params(("parallel", "arbitrary"), block_bytes,
                                scratch_bytes=3 * _nbytes((batch * (tt + CONV_K), BW_B), F32),
                                temp_bytes=16 * batch * tt * BW_B * 4),
        name="mixers",
    )(xg3, xg3, conv_w, conv_b, wa, ba, wx, bx, lam, qkvg, qkvg, qkvg, qkvg, dmat, qdec, kdec, gain)
    return ret, rec.reshape(batch * seq, W_B)


def _retention_tables(blk):
    lg = jnp.log1p(-jnp.exp2(-5.0 - jnp.arange(H_A, dtype=F32)))
    idx = jnp.arange(blk, dtype=F32)
    diff = jnp.abs(idx[:, None] - idx[None, :])
    chunk = jnp.arange(blk) // CHUNK
    visible = chunk[None, :] <= chunk[:, None]
    dmat = jnp.where(visible[None], jnp.exp(lg[:, None, None] * diff[None]), 0.0)
    qdec = jnp.exp(lg[:, None] * (idx + 1.0))
    kdec = jnp.exp(lg[:, None] * (blk - 1.0 - idx))
    bcast = lambda t: jnp.broadcast_to(t[:, :, None], (H_A, blk, DK_A))
    return dmat.astype(F32), bcast(qdec), bcast(kdec)


def _rotary_tables(seq):
    pos = jnp.arange(seq, dtype=F32)
    inv_freq = ROPE_BASE ** (-jnp.arange(0, DK_A, 2, dtype=F32) / DK_A)
    ang = pos[:, None] * inv_freq[None, :]
    cos, sin = jnp.cos(ang), jnp.sin(ang)
    return jnp.concatenate([cos, cos], axis=-1), jnp.concatenate([-sin, sin], axis=-1)


def kernel(x, w_in, ret_gn_w, conv_w, conv_b, rg_wa, rg_ba, rg_wx, rg_bx, rg_lambda, w_br_a, w_br_b,
           b_gate, w_o, ln1_g, ln1_b, w_up, w_down, ln2_g, ln2_b):
    batch, seq, d_model = x.shape
    depth = w_in.shape[0]
    d_ff = w_up.shape[-1]
    m = batch * seq
    alpha = (2.0 * depth) ** 0.25
    assert w_in.shape[-1] == 4 * W_A + 2 * W_B + 2 * d_model
    assert batch == SUBLANES, "the scan packs the batch onto the sublanes of one vreg"

    blk = _tile(seq, RET_BLOCK)
    dmat, qdec, kdec = _retention_tables(blk)
    cos2, sin2 = _rotary_tables(seq)

    bm = _tile(seq, 1024)
    bn = 1024
    assert W_A == bn and W_B == bn
    pos_blocks = seq // bm
    pos_map = lambda i, j: (i % pos_blocks, 0)
    k_scale = DK_A ** -0.5

    def rotary_epilogue(scale):
        def epilogue(accs, e_refs, o_ref, cols):
            cos_ref, sin_ref = e_refs
            c, s = cos_ref[...], sin_ref[...]
            acc = accs[0]
            for hh in range(acc.shape[1] // DK_A):
                t = acc[:, hh * DK_A:(hh + 1) * DK_A]
                rot = t * c + pltpu.roll(t, DK_A // 2, 1) * s
                if scale is not None:
                    rot = rot * scale
                lo = cols.start + hh * DK_A
                o_ref[:, lo:lo + DK_A] = rot.astype(o_ref.dtype)
        return epilogue

    def cast_epilogue(accs, e_refs, o_ref, cols):
        o_ref[:, cols] = accs[0].astype(o_ref.dtype)

    def silu_epilogue(accs, e_refs, o_ref, cols):
        o_ref[:, cols] = (accs[0] * _sigmoid(accs[0])).astype(o_ref.dtype)

    def gelu_epilogue(accs, e_refs, o_ref, cols):
        o_ref[:, cols] = _gelu_tanh(accs[0])

    def merge_epilogue(accs, e_refs, o_ref, cols):
        ga, gb, ya, yb = accs
        ba_ref, bb_ref = e_refs
        out = _sigmoid(ga + ba_ref[:, cols]) * ya + _sigmoid(gb + bb_ref[:, cols]) * yb
        o_ref[:, cols] = out.astype(o_ref.dtype)

    def relu2_epilogue(accs, e_refs, o_ref, cols):
        h = jnp.maximum(accs[0], 0.0)
        o_ref[:, cols] = (h * h).astype(o_ref.dtype)

    x32 = x.reshape(m, d_model)
    x16 = x32.astype(BF16)
    w_in_b = w_in[0].astype(BF16)
    bn_merge = 512
    bm_merge = _tile(m, 1024)
    bm_ln = _tile(m, 512)
    gate_a_off = (4 * W_A + 2 * W_B) // bn_merge
    gate_b_off = gate_a_off + d_model // bn_merge
    rot_extras = [(cos2, (bm, DK_A), pos_map), (sin2, (bm, DK_A), pos_map)]

    for l in range(depth):
        qkvg, w_br_a_b, w_br_b_b = _fused_matmul(
            "in_proj_qkvg", [x16], [(0, w_in_b, 0)], rot_extras,
            [rotary_epilogue(None), rotary_epilogue(k_scale), cast_epilogue, silu_epilogue],
            4 * W_A, BF16, bm, bn, rhs_resident=True, n_chunks=2,
            casts=[(w_br_a, l), (w_br_b, l)])
        xg, = _fused_matmul(
            "in_proj_xg", [x16], [(0, w_in_b, 4 * W_A // bn)], [],
            [cast_epilogue, gelu_epilogue], 2 * W_B, F32, bm, bn, rhs_resident=True, n_chunks=2)

        ret, rec = _mixers(qkvg, dmat, qdec, kdec, ret_gn_w[l].reshape(1, W_A),
                           xg, conv_w[l], conv_b[l].reshape(1, W_B), rg_wa[l],
                           rg_ba[l].reshape(1, W_B), rg_wx[l], rg_bx[l].reshape(1, W_B),
                           rg_lambda[l].reshape(1, W_B), batch, seq)

        bg = b_gate[l].reshape(1, 2 * d_model)
        merged, w_up_b, w_o_b = _fused_matmul(
            "merge", [x16, ret, rec],
            [(0, w_in_b, gate_a_off), (0, w_in_b, gate_b_off), (1, w_br_a_b, 0), (2, w_br_b_b, 0)],
            [(bg, (1, bn_merge), lambda i, j: (0, j)),
             (bg, (1, bn_merge), lambda i, j: (0, d_model // bn_merge + j))],
            merge_epilogue, d_model, BF16, bm_merge, bn_merge, rhs_resident=True, n_chunks=2,
            casts=[(w_up, l), (w_o, l)])

        x32, x16 = _matmul_residual_ln(
            "out_proj_ln", merged, w_o_b, x32, alpha, ln1_g[l].reshape(1, d_model),
            ln1_b[l].reshape(1, d_model), bm_ln, _tile(d_model, 1024))

        next_w_in = [(w_in, l + 1)] if l + 1 < depth else []
        hid, w_down_b, *w_in_next = _fused_matmul(
            "mlp_up", [x16], [(0, w_up_b, 0)], [], relu2_epilogue, d_ff, BF16, bm, bn,
            rhs_resident=False, n_chunks=2, casts=[(w_down, l)] + next_w_in)

        x32, x16 = _matmul_residual_ln(
            "mlp_down_ln", hid, w_down_b, x32, alpha, ln2_g[l].reshape(1, d_model),
            ln2_b[l].reshape(1, d_model), bm_ln, _tile(d_ff, 2048))
        if w_in_next:
            w_in_b = w_in_next[0]

    return x32.reshape(batch, seq, d_model)
```
